```python
import jax, jax.numpy as jnp
from jax import lax
import numpy as np


D_MODEL = 1024
BATCH = 4
SEQ = 8192
DEPTH = 2

NORM_EPS = 1e-5
N_EVEN = (DEPTH + 1) // 2
N_ODD = DEPTH // 2

POOL_WINDOWS = (2, 4, 8, 16)
N_POOL_GROUPS = len(POOL_WINDOWS)
POOL_WIDTH = D_MODEL
POOL_GC = POOL_WIDTH // N_POOL_GROUPS
CONV_WIDTH = D_MODEL
CONV_K = 3
EVEN_WIDTH = POOL_WIDTH + CONV_WIDTH
EVEN_IN = POOL_WIDTH + 3 * CONV_WIDTH + EVEN_WIDTH

HEAD_DIM = 64
N_HEADS = D_MODEL // HEAD_DIM
N_KV_HEADS = 2
GROUP = N_HEADS // N_KV_HEADS
ATTN_WIDTH = N_HEADS * HEAD_DIM
KV_WIDTH = N_KV_HEADS * HEAD_DIM
ODD_IN = ATTN_WIDTH + 2 * KV_WIDTH + ATTN_WIDTH
WINDOW = 128
Q_BLOCK = 128
ROPE_THETA = 500000.0
ROT_DIMS = HEAD_DIM // 4

kernel_name = "hybrid_pool_shortconv_swa_sink_trunk"


def rms_norm(x, g):
    x32 = x.astype(jnp.float32)
    r = x32 * lax.rsqrt(jnp.mean(x32 * x32, axis=-1, keepdims=True) + NORM_EPS)
    return (r * g.astype(jnp.float32)).astype(x.dtype)


def shift_right(u, k):
    s = u.shape[1]
    return jnp.pad(u, ((0, 0), (k, 0), (0, 0)))[:, :s]


def causal_multiscale_pool(u):
    s = u.shape[1]
    u32 = u.astype(jnp.float32)
    cs = jnp.cumsum(u32, axis=1)
    t = jnp.arange(s)
    outs = []
    for g, w in enumerate(POOL_WINDOWS):
        cg = cs[:, :, g]
        window_sum = cg - shift_right(cg, w)
        count = jnp.minimum(t + 1, w).astype(jnp.float32)[None, :, None]
        outs.append(window_sum / count - u32[:, :, g])
    return jnp.stack(outs, axis=2).astype(u.dtype)


def even_mixer(y, w_in, w_pool, pool_scale, conv_w, w_out):
    b, s, _ = y.shape
    proj = y @ w_in
    u_a, gate_b, gate_c, h_c, z = jnp.split(
        proj, [POOL_WIDTH, POOL_WIDTH + CONV_WIDTH, POOL_WIDTH + 2 * CONV_WIDTH,
               POOL_WIDTH + 3 * CONV_WIDTH], axis=-1)
    pooled = causal_multiscale_pool(u_a.reshape(b, s, N_POOL_GROUPS, POOL_GC))
    a = jnp.einsum("bsgc,gcd->bsgd", pooled, w_pool).reshape(b, s, POOL_WIDTH) * pool_scale
    cu = gate_c * h_c
    v = conv_w[2] * cu + conv_w[1] * shift_right(cu, 1) + conv_w[0] * shift_right(cu, 2)
    bo = gate_b * v
    out = jnp.concatenate([a, bo], axis=-1) * jax.nn.silu(z)
    return out @ w_out


def rope_tables(positions):
    inv_freq = ROPE_THETA ** (-jnp.arange(0, ROT_DIMS, 2, dtype=jnp.float32) / ROT_DIMS)
    ang = positions.astype(jnp.float32)[..., None] * inv_freq
    return jnp.cos(ang)[:, :, None, :], jnp.sin(ang)[:, :, None, :]


def partial_rotary(x, cos, sin):
    half = ROT_DIMS // 2
    c = cos.astype(x.dtype)
    sn = sin.astype(x.dtype)
    x1 = x[..., :half]
    x2 = x[..., half:ROT_DIMS]
    return jnp.concatenate([x1 * c - x2 * sn, x2 * c + x1 * sn, x[..., ROT_DIMS:]], axis=-1)


def band_mask(nb):
    i = jnp.arange(Q_BLOCK)[:, None]
    j = jnp.arange(2 * Q_BLOCK)[None, :]
    diff = i + Q_BLOCK - j
    band = (diff >= 0) & (diff < WINDOW)
    blk = jnp.arange(nb)[:, None, None]
    return band[None] & ((blk > 0) | (j[None] >= Q_BLOCK))


def sliding_window_gqa_sinks(q, k, v, sinks):
    b, s, _, _ = q.shape
    nb = s // Q_BLOCK
    qb = q.reshape(b, nb, Q_BLOCK, N_KV_HEADS, GROUP, HEAD_DIM)

    def with_prev(t):
        tb = t.reshape(b, nb, Q_BLOCK, N_KV_HEADS, HEAD_DIM)
        prev = jnp.pad(tb, ((0, 0), (1, 0), (0, 0), (0, 0), (0, 0)))[:, :nb]
        return jnp.concatenate([prev, tb], axis=2)

    kb = with_prev(k)
    vb = with_prev(v)
    scale = HEAD_DIM ** -0.5
    scores = jnp.einsum("bnqkgd,bnskd->bnkgqs", qb, kb,
                        preferred_element_type=jnp.float32) * scale
    mask = band_mask(nb)[None, :, None, None]
    scores = jnp.where(mask, scores, -jnp.inf)
    sink = sinks.astype(jnp.float32).reshape(1, 1, N_KV_HEADS, GROUP, 1, 1)
    m = jnp.maximum(jnp.max(scores, axis=-1, keepdims=True), sink)
    p = jnp.exp(scores - m)
    denom = jnp.sum(p, axis=-1, keepdims=True) + jnp.exp(sink - m)
    p = (p / denom).astype(v.dtype)
    out = jnp.einsum("bnkgqs,bnskd->bnqkgd", p, vb)
    return out.reshape(b, s, ATTN_WIDTH)


def odd_mixer(y, cos, sin, w_in, b_in, sinks, w_out, b_out):
    b, s, _ = y.shape
    proj = y @ w_in + b_in
    q, k, v, z = jnp.split(
        proj, [ATTN_WIDTH, ATTN_WIDTH + KV_WIDTH, ATTN_WIDTH + 2 * KV_WIDTH], axis=-1)
    q = partial_rotary(q.reshape(b, s, N_HEADS, HEAD_DIM), cos, sin)
    k = partial_rotary(k.reshape(b, s, N_KV_HEADS, HEAD_DIM), cos, sin)
    v = v.reshape(b, s, N_KV_HEADS, HEAD_DIM)
    attn = sliding_window_gqa_sinks(q, k, v, sinks)
    return (attn * jax.nn.silu(z)) @ w_out + b_out


def setup_inputs(seed: int = 0) -> dict:
    key = jax.random.key(seed)
    ks = jax.random.split(key, 16)
    nrm = jax.random.normal
    f32 = jnp.float32
    x = nrm(ks[0], (BATCH, SEQ, D_MODEL), f32)
    positions = jnp.broadcast_to(jnp.arange(SEQ, dtype=jnp.int32), (BATCH, SEQ))
    norm_g = 1.0 + 0.02 * nrm(ks[1], (DEPTH, D_MODEL), f32)
    w_in_even = nrm(ks[2], (N_EVEN, D_MODEL, EVEN_IN), f32) * D_MODEL ** -0.5
    w_pool = nrm(ks[3], (N_EVEN, N_POOL_GROUPS, POOL_GC, POOL_GC), f32) * POOL_GC ** -0.5
    pool_scale = 1.0 + 0.02 * nrm(ks[4], (N_EVEN, POOL_WIDTH), f32)
    conv_w = nrm(ks[5], (N_EVEN, CONV_K, CONV_WIDTH), f32) * CONV_K ** -0.5
    w_out_even = nrm(ks[6], (N_EVEN, EVEN_WIDTH, D_MODEL), f32) * EVEN_WIDTH ** -0.5
    w_in_odd = nrm(ks[7], (N_ODD, D_MODEL, ODD_IN), f32) * D_MODEL ** -0.5
    b_in_odd = 0.02 * nrm(ks[8], (N_ODD, ODD_IN), f32)
    attn_sinks = nrm(ks[9], (N_ODD, N_HEADS), f32)
    w_out_odd = nrm(ks[10], (N_ODD, ATTN_WIDTH, D_MODEL), f32) * ATTN_WIDTH ** -0.5
    b_out_odd = 0.02 * nrm(ks[11], (N_ODD, D_MODEL), f32)
    final_norm_g = 1.0 + 0.02 * nrm(ks[12], (D_MODEL,), f32)
    return {"x": x, "positions": positions, "norm_g": norm_g,
            "w_in_even": w_in_even, "w_pool": w_pool, "pool_scale": pool_scale,
            "conv_w": conv_w, "w_out_even": w_out_even,
            "w_in_odd": w_in_odd, "b_in_odd": b_in_odd, "attn_sinks": attn_sinks,
            "w_out_odd": w_out_odd, "b_out_odd": b_out_odd,
            "final_norm_g": final_norm_g}


def reference(x, positions, norm_g, w_in_even, w_pool, pool_scale, conv_w, w_out_even,
              w_in_odd, b_in_odd, attn_sinks, w_out_odd, b_out_odd, final_norm_g):
    cos, sin = rope_tables(positions)
    h = x
    for layer in range(DEPTH):
        y = rms_norm(h, norm_g[layer])
        i = layer // 2
        if layer % 2 == 0:
            h = h + even_mixer(y, w_in_even[i], w_pool[i], pool_scale[i], conv_w[i],
                               w_out_even[i])
        else:
            h = h + odd_mixer(y, cos, sin, w_in_odd[i], b_in_odd[i], attn_sinks[i],
                              w_out_odd[i], b_out_odd[i])
    return rms_norm(h, final_norm_g)
```

```python
import functools
import math

import jax
import jax.numpy as jnp
from jax import lax
from jax.experimental import pallas as pl
from jax.experimental.pallas import tpu as pltpu

D_MODEL = 1024
NORM_EPS = 1e-5
POOL_WINDOWS = (2, 4, 8, 16)
POOL_GC = D_MODEL // len(POOL_WINDOWS)
CONV_WIDTH = D_MODEL
EVEN_IN = 6 * D_MODEL
HEAD_DIM = 64
N_HEADS = 16
N_KV_HEADS = 2
GROUP = N_HEADS // N_KV_HEADS
KV_WIDTH = N_KV_HEADS * HEAD_DIM
ODD_IN = 2 * D_MODEL + 2 * KV_WIDTH
WINDOW = 128
ROPE_THETA = 500000.0
ROT_DIMS = HEAD_DIM // 4
N_FREQ = ROT_DIMS // 2

LANES = 128
SUBLANES = 8
POOL_CARRY = 16
CONV_CARRY = SUBLANES
PAIRS_PER_KV = GROUP // 2
LOG2E = math.log2(math.e)
Q_SCALE = HEAD_DIM ** -0.5 * LOG2E

_INV_FREQ = tuple(float(ROPE_THETA ** (-(2.0 * i) / ROT_DIMS)) for i in range(N_FREQ))

F32 = jnp.float32
BF16 = jnp.bfloat16


def _dot(a, b):
    return jnp.dot(a, b, preferred_element_type=F32)


def _rms_norm(x, g):
    ms = jnp.mean(x * x, axis=-1, keepdims=True)
    return x * lax.rsqrt(ms + NORM_EPS) * g


def _silu(z):
    return z * jax.nn.sigmoid(z)


def _rope_table_kernel(pos_ref, cos_ref, sin_ref):
    pos = pos_ref[...].astype(F32)
    for f in range(N_FREQ):
        ang = pos * _INV_FREQ[f]
        cos_ref[f] = jnp.cos(ang)
        sin_ref[f] = jnp.sin(ang)


def _rope_tables(positions):
    b, s = positions.shape
    rows = b * s // LANES
    pos2 = positions.reshape(rows, LANES)
    cos_t, sin_t = pl.pallas_call(
        _rope_table_kernel,
        out_shape=(jax.ShapeDtypeStruct((N_FREQ, rows, LANES), F32),) * 2,
        name="rope_tables",
    )(pos2)
    cos_t = cos_t.reshape(N_FREQ, b * s).T
    sin_t = sin_t.reshape(N_FREQ, b * s).T
    pad = HEAD_DIM - ROT_DIMS
    c_head = jnp.concatenate([cos_t, cos_t, jnp.ones((b * s, pad), F32)], axis=1)
    s_head = jnp.concatenate([-sin_t, sin_t, jnp.zeros((b * s, pad), F32)], axis=1)
    c_full = jnp.concatenate([c_head, c_head], axis=1).reshape(b, s, LANES)
    s_full = jnp.concatenate([s_head, s_head], axis=1).reshape(b, s, LANES)
    return c_full, s_full


def _even_kernel(x_ref, g_ref, w_in_ref, w_pool_ref, ps_ref, conv_ref, w_out_ref, o_ref,
                 u_carry, c_carry, *, tm):
    t = pl.program_id(1)

    @pl.when(t == 0)
    def _():
        u_carry[...] = jnp.zeros_like(u_carry)
        c_carry[...] = jnp.zeros_like(c_carry)

    x = x_ref[0]
    y = _rms_norm(x, g_ref[...]).astype(BF16)

    u = _dot(y, w_in_ref[:, 0:D_MODEL])
    ue = jnp.concatenate([u_carry[...], u], axis=0)
    u_carry[...] = u[tm - POOL_CARRY:, :]
    pos = t * tm + lax.broadcasted_iota(jnp.int32, (tm, LANES), 0)
    a_parts = []
    for g, w in enumerate(POOL_WINDOWS):
        lo, hi = g * POOL_GC, (g + 1) * POOL_GC
        acc = ue[:, lo:hi]
        k = 1
        while k < w:
            acc = acc + pltpu.roll(acc, k, 0)
            k *= 2
        inv = 1.0 / jnp.minimum(pos + 1, w).astype(F32)
        inv = jnp.concatenate([inv] * (POOL_GC // LANES), axis=1)
        pooled = acc[POOL_CARRY:, :] * inv - u[:, lo:hi]
        a_parts.append(_dot(pooled.astype(BF16), w_pool_ref[g]))
    a = jnp.concatenate(a_parts, axis=1) * ps_ref[...]

    gate_b = _dot(y, w_in_ref[:, D_MODEL:2 * D_MODEL])
    gate_c = _dot(y, w_in_ref[:, 2 * D_MODEL:3 * D_MODEL])
    h_c = _dot(y, w_in_ref[:, 3 * D_MODEL:4 * D_MODEL])
    cu = gate_c * h_c
    ce = jnp.concatenate([c_carry[...], cu], axis=0)
    c_carry[...] = cu[tm - CONV_CARRY:, :]
    cw = conv_ref[...]
    v = (cw[2:3, :] * cu
         + cw[1:2, :] * pltpu.roll(ce, 1, 0)[CONV_CARRY:, :]
         + cw[0:1, :] * pltpu.roll(ce, 2, 0)[CONV_CARRY:, :])
    bo = gate_b * v

    z = _dot(y, w_in_ref[:, 4 * D_MODEL:6 * D_MODEL])
    mix = jnp.concatenate([a, bo], axis=1) * _silu(z)
    o_ref[0] = x + _dot(mix.astype(BF16), w_out_ref[...])


def _const_spec(shape):
    nd = len(shape)
    return pl.BlockSpec(shape, lambda *_: (0,) * nd, pipeline_mode=pl.Buffered(1))


def _even_layer(x, g, w_in, w_pool, pool_scale, conv_w, w_out, *, tm):
    b, s, d = x.shape
    tile = pl.BlockSpec((1, tm, d), lambda i, j: (i, j, 0))
    weight_bytes = 2 * (w_in.size + w_pool.size + w_out.size)
    tile_bytes = 4 * tm * d
    vmem_limit = weight_bytes + (4 + 16) * tile_bytes + (4 << 20)
    return pl.pallas_call(
        functools.partial(_even_kernel, tm=tm),
        grid=(b, s // tm),
        in_specs=[tile, _const_spec((1, d)), _const_spec(w_in.shape), _const_spec(w_pool.shape),
                  _const_spec((1, d)), _const_spec(conv_w.shape), _const_spec(w_out.shape)],
        out_specs=tile,
        out_shape=jax.ShapeDtypeStruct(x.shape, x.dtype),
        scratch_shapes=[pltpu.VMEM((POOL_CARRY, d), F32), pltpu.VMEM((CONV_CARRY, d), F32)],
        compiler_params=pltpu.CompilerParams(
            dimension_semantics=("arbitrary", "arbitrary"), vmem_limit_bytes=vmem_limit),
        name="even_layer",
    )(x, g.reshape(1, d), w_in, w_pool, pool_scale.reshape(1, d), conv_w, w_out)


def _odd_kernel(sink_ref, h_ref, c_ref, s_ref, g_ref, w_in_ref, b_in_ref, w_out_ref, b_out_ref, gf_ref,
                o_ref, kx, vx, *, tm):
    t = pl.program_id(1)
    nq = tm // WINDOW

    @pl.when(t == 0)
    def _():
        kx[:, 0:WINDOW, :] = jnp.zeros((2 * N_KV_HEADS, WINDOW, LANES), BF16)
        vx[:, 0:WINDOW, :] = jnp.zeros((2 * N_KV_HEADS, WINDOW, LANES), BF16)

    h = h_ref[0]
    y = _rms_norm(h, g_ref[...]).astype(BF16)
    proj = _dot(y, w_in_ref[...]) + b_in_ref[...]
    k = proj[:, D_MODEL:D_MODEL + KV_WIDTH]
    v = proj[:, D_MODEL + KV_WIDTH:D_MODEL + 2 * KV_WIDTH]
    z = proj[:, D_MODEL + 2 * KV_WIDTH:]

    c_tab = c_ref[0]
    s_tab = s_ref[0]
    lane = lax.broadcasted_iota(jnp.int32, (tm, LANES), 1)
    first_half = (lane & (HEAD_DIM - 1)) < N_FREQ

    def rope(xc, cc, sc):
        swapped = jnp.where(first_half, pltpu.roll(xc, LANES - N_FREQ, 1), pltpu.roll(xc, N_FREQ, 1))
        return xc * cc + swapped * sc

    cq = c_tab * Q_SCALE
    sq = s_tab * Q_SCALE
    qr = [rope(proj[:, j * LANES:(j + 1) * LANES], cq, sq).astype(BF16) for j in range(D_MODEL // LANES)]
    kr = rope(k, c_tab, s_tab)

    head0 = lane < HEAD_DIM

    def spread(x, ref):
        lo0 = jnp.where(head0, x, 0.0)
        hi1 = jnp.where(head0, 0.0, x)
        ref[0, WINDOW:, :] = lo0.astype(BF16)
        ref[1, WINDOW:, :] = pltpu.roll(lo0, HEAD_DIM, 1).astype(BF16)
        ref[2, WINDOW:, :] = pltpu.roll(hi1, HEAD_DIM, 1).astype(BF16)
        ref[3, WINDOW:, :] = hi1.astype(BF16)

    spread(kr, kx)
    spread(v, vx)

    r2 = lax.broadcasted_iota(jnp.int32, (WINDOW, LANES), 0)
    l2 = lax.broadcasted_iota(jnp.int32, (WINDOW, LANES), 1)
    diff = l2 - r2
    valid_cur = diff <= 0
    neg_inf = jnp.float32(-jnp.inf)
    rr = lax.broadcasted_iota(jnp.int32, (4 * WINDOW, LANES), 0)
    ll = lax.broadcasted_iota(jnp.int32, (4 * WINDOW, LANES), 1)
    ones_cat = jnp.where((rr < 2 * WINDOW) == (ll < HEAD_DIM), 1.0, 0.0).astype(BF16)
    r16 = lax.broadcasted_iota(jnp.int32, (2 * SUBLANES, LANES), 0)
    drop_row0 = jnp.where(r16 == 0, 0.0, 1.0).astype(BF16)

    out_chunks = [[None] * nq for _ in range(D_MODEL // LANES)]
    for qb in range(nq):
        a0 = qb * WINDOW
        if qb == 0:
            thr = jnp.where(t == 0, LANES, 0)
            valid_prev = diff > thr
        else:
            valid_prev = diff > 0
        for kv in range(N_KV_HEADS):
            w_cat = jnp.concatenate([kx[2 * kv, a0:a0 + 2 * WINDOW, :], kx[2 * kv + 1, a0:a0 + 2 * WINDOW, :]],
                                    axis=0)
            q_st = jnp.concatenate([qr[kv * PAIRS_PER_KV + j][a0:a0 + WINDOW, :] for j in range(PAIRS_PER_KV)],
                                   axis=0)
            sc = lax.dot_general(q_st, w_cat, (((1,), (1,)), ((), ())), preferred_element_type=F32)
            p_rows = []
            for j in range(PAIRS_PER_KV):
                blocks = []
                for par in range(2):
                    head = kv * GROUP + 2 * j + par
                    fill = jnp.where(l2 == 0, sink_ref[head] * LOG2E, neg_inf)
                    s0 = sc[j * WINDOW:(j + 1) * WINDOW, par * 2 * WINDOW:par * 2 * WINDOW + WINDOW]
                    s1 = sc[j * WINDOW:(j + 1) * WINDOW, par * 2 * WINDOW + WINDOW:(par + 1) * 2 * WINDOW]
                    m0 = jnp.where(valid_prev, s0, fill)
                    m1 = jnp.where(valid_cur, s1, neg_inf)
                    m = jnp.max(jnp.maximum(m0, m1), axis=-1, keepdims=True)
                    blocks.append(jnp.exp2(m0 - m).astype(BF16))
                    blocks.append(jnp.exp2(m1 - m).astype(BF16))
                p_rows.append(jnp.concatenate(blocks, axis=1))
            p = jnp.concatenate(p_rows, axis=0)

            def vwin(idx):
                top = vx[idx, a0:a0 + 2 * SUBLANES, :] * drop_row0
                return jnp.concatenate([top, vx[idx, a0 + 2 * SUBLANES:a0 + 2 * WINDOW, :]], axis=0)

            v_cat = jnp.concatenate(
                [jnp.concatenate([vwin(2 * kv), vwin(2 * kv + 1)], axis=0), ones_cat], axis=1)
            o = _dot(p, v_cat)
            o = o[:, :LANES] * (1.0 / o[:, LANES:])
            for j in range(PAIRS_PER_KV):
                out_chunks[kv * PAIRS_PER_KV + j][qb] = o[j * WINDOW:(j + 1) * WINDOW, :]

    kx[:, 0:WINDOW, :] = kx[:, tm:tm + WINDOW, :]
    vx[:, 0:WINDOW, :] = vx[:, tm:tm + WINDOW, :]

    attn = jnp.concatenate([jnp.concatenate(col, axis=0) for col in out_chunks], axis=1)
    gated = (attn * _silu(z)).astype(BF16)
    h2 = h + _dot(gated, w_out_ref[...]) + b_out_ref[...]
    o_ref[0] = _rms_norm(h2, gf_ref[...])


def _odd_layer(h, c_full, s_full, g, w_in, b_in, sinks, w_out, b_out, g_final, *, tm):
    b, s, d = h.shape
    tile = pl.BlockSpec((1, tm, d), lambda i, j: (i, j, 0))
    tab = pl.BlockSpec((1, tm, LANES), lambda i, j: (i, j, 0))
    weight_bytes = 2 * (w_in.size + w_out.size)
    tile_bytes = 4 * tm * d
    vmem_limit = weight_bytes + (4 + 12) * tile_bytes + (8 << 20)
    return pl.pallas_call(
        functools.partial(_odd_kernel, tm=tm),
        grid=(b, s // tm),
        in_specs=[pl.BlockSpec(memory_space=pltpu.SMEM), tile, tab, tab, _const_spec((1, d)),
                  _const_spec(w_in.shape), _const_spec((1, ODD_IN)), _const_spec(w_out.shape),
                  _const_spec((1, d)), _const_spec((1, d))],
        out_specs=tile,
        out_shape=jax.ShapeDtypeStruct(h.shape, h.dtype),
        scratch_shapes=[pltpu.VMEM((2 * N_KV_HEADS, WINDOW + tm, LANES), BF16),
                        pltpu.VMEM((2 * N_KV_HEADS, WINDOW + tm, LANES), BF16)],
        compiler_params=pltpu.CompilerParams(
            dimension_semantics=("arbitrary", "arbitrary"), vmem_limit_bytes=vmem_limit),
        name="odd_layer",
    )(sinks, h, c_full, s_full, g.reshape(1, d), w_in, b_in.reshape(1, ODD_IN), w_out,
      b_out.reshape(1, d), g_final.reshape(1, d))


EVEN_TM = 256
ODD_TM = 256


def kernel(x, positions, norm_g, w_in_even, w_pool, pool_scale, conv_w, w_out_even, w_in_odd, b_in_odd,
           attn_sinks, w_out_odd, b_out_odd, final_norm_g):
    assert norm_g.shape[0] == 2 and w_in_even.shape[0] == 1 and w_in_odd.shape[0] == 1
    c_full, s_full = _rope_tables(positions)
    h = _even_layer(x, norm_g[0], w_in_even[0].astype(BF16), w_pool[0].astype(BF16), pool_scale[0],
                    conv_w[0], w_out_even[0].astype(BF16), tm=EVEN_TM)
    return _odd_layer(h, c_full, s_full, norm_g[1], w_in_odd[0].astype(BF16), b_in_odd[0], attn_sinks[0],
                      w_out_odd[0].astype(BF16), b_out_odd[0], final_norm_g, tm=ODD_TM)
```

```python
import functools
import math

import jax
import jax.numpy as jnp
from jax import lax
from jax.experimental import pallas as pl
from jax.experimental.pallas import tpu as pltpu

D_MODEL = 1024
NORM_EPS = 1e-5
POOL_WINDOWS = (2, 4, 8, 16)
POOL_GC = D_MODEL // len(POOL_WINDOWS)
CONV_WIDTH = D_MODEL
EVEN_IN = 6 * D_MODEL
HEAD_DIM = 64
N_HEADS = 16
N_KV_HEADS = 2
GROUP = N_HEADS // N_KV_HEADS
KV_WIDTH = N_KV_HEADS * HEAD_DIM
ODD_IN = 2 * D_MODEL + 2 * KV_WIDTH
WINDOW = 128
ROPE_THETA = 500000.0
ROT_DIMS = HEAD_DIM // 4
N_FREQ = ROT_DIMS // 2

LANES = 128
SUBLANES = 8
POOL_CARRY = 16
CONV_CARRY = SUBLANES
PAIRS_PER_KV = GROUP // 2
LOG2E = math.log2(math.e)
Q_SCALE = HEAD_DIM ** -0.5 * LOG2E

_INV_FREQ = tuple(float(ROPE_THETA ** (-(2.0 * i) / ROT_DIMS)) for i in range(N_FREQ))

F32 = jnp.float32
BF16 = jnp.bfloat16


def _dot(a, b):
    return jnp.dot(a, b, preferred_element_type=F32)


def _rms_norm(x, g):
    ms = jnp.mean(x * x, axis=-1, keepdims=True)
    return x * lax.rsqrt(ms + NORM_EPS) * g


def _silu(z):
    return z * jax.nn.sigmoid(z)


def _rope_tables(pos_row, tm):
    sub = lax.broadcasted_iota(jnp.int32, (N_FREQ, tm), 0)
    inv_freq = jnp.full((N_FREQ, tm), _INV_FREQ[N_FREQ - 1], F32)
    for f in range(N_FREQ - 1):
        inv_freq = jnp.where(sub == f, _INV_FREQ[f], inv_freq)
    ang = pos_row.astype(F32) * inv_freq
    cos8 = jnp.cos(ang)
    sin8 = jnp.sin(ang)
    r = lax.broadcasted_iota(jnp.int32, (LANES, LANES), 0) & (HEAD_DIM - 1)
    c_blocks, s_blocks = [], []
    for blk in range(tm // LANES):
        c_rep = jnp.concatenate([cos8[:, blk * LANES:(blk + 1) * LANES]] * (LANES // N_FREQ), axis=0)
        s_rep = jnp.concatenate([sin8[:, blk * LANES:(blk + 1) * LANES]] * (LANES // N_FREQ), axis=0)
        c_pat = jnp.where(r < ROT_DIMS, c_rep, 1.0)
        s_pat = jnp.where(r < N_FREQ, -s_rep, jnp.where(r < ROT_DIMS, s_rep, 0.0))
        c_blocks.append(c_pat.T)
        s_blocks.append(s_pat.T)
    return jnp.concatenate(c_blocks, axis=0), jnp.concatenate(s_blocks, axis=0)


def _even_kernel(x_ref, g_ref, w_in_ref, w_pool_ref, ps_ref, conv_ref, w_out_ref, o_ref,
                 u_carry, c_carry, *, tm):
    t = pl.program_id(1)

    @pl.when(t == 0)
    def _():
        u_carry[...] = jnp.zeros_like(u_carry)
        c_carry[...] = jnp.zeros_like(c_carry)

    x = x_ref[0]
    y = _rms_norm(x, g_ref[...]).astype(BF16)

    u = _dot(y, w_in_ref[:, 0:D_MODEL])
    ue = jnp.concatenate([u_carry[...], u], axis=0)
    u_carry[...] = u[tm - POOL_CARRY:, :]
    pos = t * tm + lax.broadcasted_iota(jnp.int32, (tm, LANES), 0)
    a_parts = []
    for g, w in enumerate(POOL_WINDOWS):
        lo, hi = g * POOL_GC, (g + 1) * POOL_GC
        acc = ue[:, lo:hi]
        k = 1
        while k < w:
            acc = acc + pltpu.roll(acc, k, 0)
            k *= 2
        inv = 1.0 / jnp.minimum(pos + 1, w).astype(F32)
        inv = jnp.concatenate([inv] * (POOL_GC // LANES), axis=1)
        pooled = acc[POOL_CARRY:, :] * inv - u[:, lo:hi]
        a_parts.append(_dot(pooled.astype(BF16), w_pool_ref[g]))
    a = jnp.concatenate(a_parts, axis=1) * ps_ref[...]

    gate_b = _dot(y, w_in_ref[:, D_MODEL:2 * D_MODEL])
    gate_c = _dot(y, w_in_ref[:, 2 * D_MODEL:3 * D_MODEL])
    h_c = _dot(y, w_in_ref[:, 3 * D_MODEL:4 * D_MODEL])
    cu = gate_c * h_c
    ce = jnp.concatenate([c_carry[...], cu], axis=0)
    c_carry[...] = cu[tm - CONV_CARRY:, :]
    cw = conv_ref[...]
    v = (cw[2:3, :] * cu
         + cw[1:2, :] * pltpu.roll(ce, 1, 0)[CONV_CARRY:, :]
         + cw[0:1, :] * pltpu.roll(ce, 2, 0)[CONV_CARRY:, :])
    bo = gate_b * v

    z = _dot(y, w_in_ref[:, 4 * D_MODEL:6 * D_MODEL])
    mix = jnp.concatenate([a, bo], axis=1) * _silu(z)
    o_ref[0] = x + _dot(mix.astype(BF16), w_out_ref[...])


def _const_spec(shape):
    nd = len(shape)
    return pl.BlockSpec(shape, lambda *_: (0,) * nd, pipeline_mode=pl.Buffered(1))


def _even_layer(x, g, w_in, w_pool, pool_scale, conv_w, w_out, *, tm):
    b, s, d = x.shape
    tile = pl.BlockSpec((1, tm, d), lambda i, j: (i, j, 0))
    weight_bytes = 2 * (w_in.size + w_pool.size + w_out.size)
    tile_bytes = 4 * tm * d
    vmem_limit = weight_bytes + (4 + 16) * tile_bytes + (4 << 20)
    return pl.pallas_call(
        functools.partial(_even_kernel, tm=tm),
        grid=(b, s // tm),
        in_specs=[tile, _const_spec((1, d)), _const_spec(w_in.shape), _const_spec(w_pool.shape),
                  _const_spec((1, d)), _const_spec(conv_w.shape), _const_spec(w_out.shape)],
        out_specs=tile,
        out_shape=jax.ShapeDtypeStruct(x.shape, x.dtype),
        scratch_shapes=[pltpu.VMEM((POOL_CARRY, d), F32), pltpu.VMEM((CONV_CARRY, d), F32)],
        compiler_params=pltpu.CompilerParams(
            dimension_semantics=("arbitrary", "arbitrary"), vmem_limit_bytes=vmem_limit),
        name="even_layer",
    )(x, g.reshape(1, d), w_in, w_pool, pool_scale.reshape(1, d), conv_w, w_out)


def _odd_kernel(sink_ref, h_ref, pos_ref, g_ref, w_in_ref, b_in_ref, w_out_ref, b_out_ref, gf_ref,
                o_ref, kx, vx, *, tm):
    t = pl.program_id(1)
    nq = tm // WINDOW

    @pl.when(t == 0)
    def _():
        kx[:, 0:WINDOW, :] = jnp.zeros((2 * N_KV_HEADS, WINDOW, LANES), BF16)
        vx[:, 0:WINDOW, :] = jnp.zeros((2 * N_KV_HEADS, WINDOW, LANES), BF16)

    h = h_ref[0]
    y = _rms_norm(h, g_ref[...]).astype(BF16)
    proj = _dot(y, w_in_ref[...]) + b_in_ref[...]
    k = proj[:, D_MODEL:D_MODEL + KV_WIDTH]
    v = proj[:, D_MODEL + KV_WIDTH:D_MODEL + 2 * KV_WIDTH]
    z = proj[:, D_MODEL + 2 * KV_WIDTH:]

    c_tab, s_tab = _rope_tables(pos_ref[0], tm)
    lane =lax.broadcasted_iota(jnp.int32, (tm, LANES), 1)
    first_half = (lane & (HEAD_DIM - 1)) < N_FREQ

    def rope(xc, cc, sc):
        swapped = jnp.where(first_half, pltpu.roll(xc, LANES - N_FREQ, 1), pltpu.roll(xc, N_FREQ, 1))
        return xc * cc + swapped * sc

    cq = c_tab * Q_SCALE
    sq = s_tab * Q_SCALE
    qr = [rope(proj[:, j * LANES:(j + 1) * LANES], cq, sq).astype(BF16) for j in range(D_MODEL // LANES)]
    kr = rope(k, c_tab, s_tab)

    head0 = lane < HEAD_DIM

    def spread(x, ref):
        lo0 = jnp.where(head0, x, 0.0)
        hi1 = jnp.where(head0, 0.0, x)
        ref[0, WINDOW:, :] = lo0.astype(BF16)
        ref[1, WINDOW:, :] = pltpu.roll(lo0, HEAD_DIM, 1).astype(BF16)
        ref[2, WINDOW:, :] = pltpu.roll(hi1, HEAD_DIM, 1).astype(BF16)
        ref[3, WINDOW:, :] = hi1.astype(BF16)

    spread(kr, kx)
    spread(v, vx)

    r2 = lax.broadcasted_iota(jnp.int32, (WINDOW, LANES), 0)
    l2 = lax.broadcasted_iota(jnp.int32, (WINDOW, LANES), 1)
    diff = l2 - r2
    valid_cur = diff <= 0
    neg_inf = jnp.float32(-jnp.inf)
    rr = lax.broadcasted_iota(jnp.int32, (4 * WINDOW, LANES), 0)
    ll = lax.broadcasted_iota(jnp.int32, (4 * WINDOW, LANES), 1)
    ones_cat = jnp.where((rr < 2 * WINDOW) == (ll < HEAD_DIM), 1.0, 0.0).astype(BF16)
    r16 = lax.broadcasted_iota(jnp.int32, (2 * SUBLANES, LANES), 0)
    drop_row0 = jnp.where(r16 == 0, 0.0, 1.0).astype(BF16)

    out_chunks = [[None] * nq for _ in range(D_MODEL // LANES)]
    for qb in range(nq):
        a0 = qb * WINDOW
        if qb == 0:
            thr = jnp.where(t == 0, LANES, 0)
            valid_prev = diff > thr
        else:
            valid_prev = diff > 0
        for kv in range(N_KV_HEADS):
            w_cat = jnp.concatenate([kx[2 * kv, a0:a0 + 2 * WINDOW, :], kx[2 * kv + 1, a0:a0 + 2 * WINDOW, :]],
                                    axis=0)
            q_st = jnp.concatenate([qr[kv * PAIRS_PER_KV + j][a0:a0 + WINDOW, :] for j in range(PAIRS_PER_KV)],
                                   axis=0)
            sc = lax.dot_general(q_st, w_cat, (((1,), (1,)), ((), ())), preferred_element_type=F32)
            p_rows = []
            for j in range(PAIRS_PER_KV):
                blocks = []
                for par in range(2):
                    head = kv * GROUP + 2 * j + par
                    fill = jnp.where(l2 == 0, sink_ref[head] * LOG2E, neg_inf)
                    s0 = sc[j * WINDOW:(j + 1) * WINDOW, par * 2 * WINDOW:par * 2 * WINDOW + WINDOW]
                    s1 = sc[j * WINDOW:(j + 1) * WINDOW, par * 2 * WINDOW + WINDOW:(par + 1) * 2 * WINDOW]
                    m0 = jnp.where(valid_prev, s0, fill)
                    m1 = jnp.where(valid_cur, s1, neg_inf)
                    m = jnp.max(jnp.maximum(m0, m1), axis=-1, keepdims=True)
                    blocks.append(jnp.exp2(m0 - m).astype(BF16))
                    blocks.append(jnp.exp2(m1 - m).astype(BF16))
                p_rows.append(jnp.concatenate(blocks, axis=1))
            p = jnp.concatenate(p_rows, axis=0)

            def vwin(idx):
                top = vx[idx, a0:a0 + 2 * SUBLANES, :] * drop_row0
                return jnp.concatenate([top, vx[idx, a0 + 2 * SUBLANES:a0 + 2 * WINDOW, :]], axis=0)

            v_cat = jnp.concatenate(
                [jnp.concatenate([vwin(2 * kv), vwin(2 * kv + 1)], axis=0), ones_cat], axis=1)
            o = _dot(p, v_cat)
            o = o[:, :LANES] * (1.0 / o[:, LANES:])
            for j in range(PAIRS_PER_KV):
                out_chunks[kv * PAIRS_PER_KV + j][qb] = o[j * WINDOW:(j + 1) * WINDOW, :]

    kx[:, 0:WINDOW, :] = kx[:, tm:tm + WINDOW, :]
    vx[:, 0:WINDOW, :] = vx[:, tm:tm + WINDOW, :]

    attn = jnp.concatenate([jnp.concatenate(col, axis=0) for col in out_chunks], axis=1)
    gated = (attn * _silu(z)).astype(BF16)
    h2 = h + _dot(gated, w_out_ref[...]) + b_out_ref[...]
    o_ref[0] = _rms_norm(h2, gf_ref[...])


def _odd_layer(h, positions, g, w_in, b_in, sinks, w_out, b_out, g_final, *, tm):
    b, s, d = h.shape
    tile = pl.BlockSpec((1, tm, d), lambda i, j: (i, j, 0))
    pos_spec = pl.BlockSpec((1, 1, tm), lambda i, j: (i, 0, j))
    weight_bytes = 2 * (w_in.size + w_out.size)
    tile_bytes = 4 * tm * d
    vmem_limit = weight_bytes + (4 + 12) * tile_bytes + (8 << 20)
    return pl.pallas_call(
        functools.partial(_odd_kernel, tm=tm),
        grid=(b, s // tm),
        in_specs=[pl.BlockSpec(memory_space=pltpu.SMEM), tile, pos_spec, _const_spec((1, d)),
                  _const_spec(w_in.shape), _const_spec((1, ODD_IN)), _const_spec(w_out.shape),
                  _const_spec((1, d)), _const_spec((1, d))],
        out_specs=tile,
        out_shape=jax.ShapeDtypeStruct(h.shape, h.dtype),
        scratch_shapes=[pltpu.VMEM((2 * N_KV_HEADS, WINDOW + tm, LANES), BF16),
                        pltpu.VMEM((2 * N_KV_HEADS, WINDOW + tm, LANES), BF16)],
        compiler_params=pltpu.CompilerParams(
            dimension_semantics=("arbitrary", "arbitrary"), vmem_limit_bytes=vmem_limit),
        name="odd_layer",
    )(sinks, h, positions.reshape(b, 1, s), g.reshape(1, d), w_in, b_in.reshape(1, ODD_IN), w_out,
      b_out.reshape(1, d), g_final.reshape(1, d))


EVEN_TM = 256
ODD_TM = 256


def kernel(x, positions, norm_g, w_in_even, w_pool, pool_scale, conv_w, w_out_even, w_in_odd, b_in_odd,
           attn_sinks, w_out_odd, b_out_odd, final_norm_g):
    assert norm_g.shape[0] == 2 and w_in_even.shape[0] == 1 and w_in_odd.shape[0] == 1
    h = _even_layer(x, norm_g[0], w_in_even[0].astype(BF16), w_pool[0].astype(BF16), pool_scale[0],
                    conv_w[0], w_out_even[0].astype(BF16), tm=EVEN_TM)
    return _odd_layer(h, positions, norm_g[1], w_in_odd[0].astype(BF16), b_in_odd[0], attn_sinks[0],
                      w_out_odd[0].astype(BF16), b_out_odd[0], final_norm_g, tm=ODD_TM)
```

```python
import functools
import math

import jax
import jax.numpy as jnp
from jax import lax
from jax.experimental import pallas as pl
from jax.experimental.pallas import tpu as pltpu

D_MODEL = 1024
NORM_EPS = 1e-5
POOL_WINDOWS = (2, 4, 8, 16)
POOL_GC = D_MODEL // len(POOL_WINDOWS)
CONV_WIDTH = D_MODEL
EVEN_IN = 6 * D_MODEL
HEAD_DIM = 64
N_HEADS = 16
N_KV_HEADS = 2
GROUP = N_HEADS // N_KV_HEADS
KV_WIDTH = N_KV_HEADS * HEAD_DIM
ODD_IN = 2 * D_MODEL + 2 * KV_WIDTH
WINDOW = 128
ROPE_THETA = 500000.0
ROT_DIMS = HEAD_DIM // 4
N_FREQ = ROT_DIMS // 2

LANES = 128
SUBLANES = 8
POOL_CARRY = 16
CONV_CARRY = SUBLANES
PAIRS_PER_KV = GROUP // 2
LOG2E = math.log2(math.e)
Q_SCALE = HEAD_DIM ** -0.5 * LOG2E
MASK_BIAS = 2.0 ** 100

_INV_FREQ = tuple(float(ROPE_THETA ** (-(2.0 * i) / ROT_DIMS)) for i in range(N_FREQ))

F32 = jnp.float32
BF16 = jnp.bfloat16


def _dot(a, b):
    return jnp.dot(a, b, preferred_element_type=F32)


def _rms_norm(x, g):
    ms = jnp.mean(x * x, axis=-1, keepdims=True)
    return x * lax.rsqrt(ms + NORM_EPS) * g


def _silu(z):
    return z * jax.nn.sigmoid(z)


def _rope_tables(pos_row, tm):
    sub = lax.broadcasted_iota(jnp.int32, (N_FREQ, tm), 0)
    inv_freq = jnp.full((N_FREQ, tm), _INV_FREQ[N_FREQ - 1], F32)
    for f in range(N_FREQ - 1):
        inv_freq = jnp.where(sub == f, _INV_FREQ[f], inv_freq)
    ang = pos_row.astype(F32) * inv_freq
    cos8 = jnp.cos(ang)
    sin8 = jnp.sin(ang)
    r = lax.broadcasted_iota(jnp.int32, (LANES, LANES), 0) & (HEAD_DIM - 1)
    c_blocks, s_blocks = [], []
    for blk in range(tm // LANES):
        c_rep = jnp.concatenate([cos8[:, blk * LANES:(blk + 1) * LANES]] * (LANES // N_FREQ), axis=0)
        s_rep = jnp.concatenate([sin8[:, blk * LANES:(blk + 1) * LANES]] * (LANES // N_FREQ), axis=0)
        c_pat = jnp.where(r < ROT_DIMS, c_rep, 1.0)
        s_pat = jnp.where(r < N_FREQ, -s_rep, jnp.where(r < ROT_DIMS, s_rep, 0.0))
        c_blocks.append(c_pat.T)
        s_blocks.append(s_pat.T)
    return jnp.concatenate(c_blocks, axis=0), jnp.concatenate(s_blocks, axis=0)


def _even_kernel(x_ref, g_ref, w_in_ref, w_pool_ref, ps_ref, conv_ref, w_out_ref, o_ref,
                 u_carry, c_carry, *, tm):
    t = pl.program_id(1)

    @pl.when(t == 0)
    def _():
        u_carry[...] = jnp.zeros_like(u_carry)
        c_carry[...] = jnp.zeros_like(c_carry)

    x = x_ref[0]
    y = _rms_norm(x, g_ref[...]).astype(BF16)

    u = _dot(y, w_in_ref[:, 0:D_MODEL])
    ue = jnp.concatenate([u_carry[...], u], axis=0)
    u_carry[...] = u[tm - POOL_CARRY:, :]
    pos = t * tm + lax.broadcasted_iota(jnp.int32, (tm, LANES), 0)
    a_parts = []
    for g, w in enumerate(POOL_WINDOWS):
        lo, hi = g * POOL_GC, (g + 1) * POOL_GC
        acc = ue[:, lo:hi]
        k = 1
        while k < w:
            acc = acc + pltpu.roll(acc, k, 0)
            k *= 2
        inv = 1.0 / jnp.minimum(pos + 1, w).astype(F32)
        inv = jnp.concatenate([inv] * (POOL_GC // LANES), axis=1)
        pooled = acc[POOL_CARRY:, :] * inv - u[:, lo:hi]
        a_parts.append(_dot(pooled.astype(BF16), w_pool_ref[g]))
    a = jnp.concatenate(a_parts, axis=1) * ps_ref[...]

    gate_b = _dot(y, w_in_ref[:, D_MODEL:2 * D_MODEL])
    gate_c = _dot(y, w_in_ref[:, 2 * D_MODEL:3 * D_MODEL])
    h_c = _dot(y, w_in_ref[:, 3 * D_MODEL:4 * D_MODEL])
    cu = gate_c * h_c
    ce = jnp.concatenate([c_carry[...], cu], axis=0)
    c_carry[...] = cu[tm - CONV_CARRY:, :]
    cw = conv_ref[...]
    v = (cw[2:3, :] * cu
         + cw[1:2, :] * pltpu.roll(ce, 1, 0)[CONV_CARRY:, :]
         + cw[0:1, :] * pltpu.roll(ce, 2, 0)[CONV_CARRY:, :])
    bo = gate_b * v

    z = _dot(y, w_in_ref[:, 4 * D_MODEL:6 * D_MODEL])
    mix = jnp.concatenate([a, bo], axis=1) * _silu(z)
    o_ref[0] = x + _dot(mix.astype(BF16), w_out_ref[...])


def _const_spec(shape):
    nd = len(shape)
    return pl.BlockSpec(shape, lambda *_: (0,) * nd, pipeline_mode=pl.Buffered(1))


def _even_layer(x, g, w_in, w_pool, pool_scale, conv_w, w_out, *, tm):
    b, s, d = x.shape
    tile = pl.BlockSpec((1, tm, d), lambda i, j: (i, j, 0))
    weight_bytes = 2 * (w_in.size + w_pool.size + w_out.size)
    tile_bytes = 4 * tm * d
    vmem_limit = weight_bytes + (4 + 16) * tile_bytes + (4 << 20)
    return pl.pallas_call(
        functools.partial(_even_kernel, tm=tm),
        grid=(b, s // tm),
        in_specs=[tile, _const_spec((1, d)), _const_spec(w_in.shape), _const_spec(w_pool.shape),
                  _const_spec((1, d)), _const_spec(conv_w.shape), _const_spec(w_out.shape)],
        out_specs=tile,
        out_shape=jax.ShapeDtypeStruct(x.shape, x.dtype),
        scratch_shapes=[pltpu.VMEM((POOL_CARRY, d), F32), pltpu.VMEM((CONV_CARRY, d), F32)],
        compiler_params=pltpu.CompilerParams(
            dimension_semantics=("arbitrary", "arbitrary"), vmem_limit_bytes=vmem_limit),
        name="even_layer",
    )(x, g.reshape(1, d), w_in, w_pool, pool_scale.reshape(1, d), conv_w, w_out)


def _odd_kernel(sink_ref, h_ref, pos_ref, g_ref, w_in_ref, b_in_ref, w_out_ref, b_out_ref, gf_ref,
                o_ref, kx, vtx, *, tm):
    t = pl.program_id(1)
    nq = tm // WINDOW

    @pl.when(t == 0)
    def _():
        kx[:, 0:WINDOW, :] = jnp.zeros((2 * N_KV_HEADS, WINDOW, LANES), BF16)
        vtx[:, 0:WINDOW] = jnp.zeros((KV_WIDTH, WINDOW), BF16)

    h = h_ref[0]
    y = _rms_norm(h, g_ref[...]).astype(BF16)
    proj = _dot(y, w_in_ref[...]) + b_in_ref[...]
    k = proj[:, D_MODEL:D_MODEL + KV_WIDTH]
    v = proj[:, D_MODEL + KV_WIDTH:D_MODEL + 2 * KV_WIDTH]
    z = proj[:, D_MODEL + 2 * KV_WIDTH:]

    c_tab, s_tab = _rope_tables(pos_ref[0], tm)
    lane = lax.broadcasted_iota(jnp.int32, (tm, LANES), 1)
    first_half = (lane & (HEAD_DIM - 1)) < N_FREQ

    def rope(xc, cc, sc):
        swapped = jnp.where(first_half, pltpu.roll(xc, LANES - N_FREQ, 1), pltpu.roll(xc, N_FREQ, 1))
        return xc * cc + swapped * sc

    cq = c_tab * Q_SCALE
    sq = s_tab * Q_SCALE
    qr = [rope(proj[:, j * LANES:(j + 1) * LANES], cq, sq).astype(BF16) for j in range(D_MODEL // LANES)]
    kr = rope(k, c_tab, s_tab)

    head0 = lane < HEAD_DIM
    lo0 = jnp.where(head0, kr, 0.0)
    hi1 = jnp.where(head0, 0.0, kr)
    kx[0, WINDOW:, :] = lo0.astype(BF16)
    kx[1, WINDOW:, :] = pltpu.roll(lo0, HEAD_DIM, 1).astype(BF16)
    kx[2, WINDOW:, :] = pltpu.roll(hi1, HEAD_DIM, 1).astype(BF16)
    kx[3, WINDOW:, :] = hi1.astype(BF16)
    for blk in range(nq):
        vtx[:, WINDOW + blk * WINDOW:WINDOW + (blk + 1) * WINDOW] = (
            v[blk * WINDOW:(blk + 1) * WINDOW, :].T.astype(BF16))

    slot = lax.broadcasted_iota(jnp.int32, (WINDOW, LANES), 0)
    qidx = lax.broadcasted_iota(jnp.int32, (WINDOW, LANES), 1)
    diff = slot - qidx
    eye = jnp.where(diff == 0, 1.0, 0.0).astype(BF16)
    masked = jnp.full((WINDOW, LANES), -MASK_BIAS, F32)
    bias_prev = jnp.where(diff > 0, 0.0, masked)
    bias_cur = jnp.where(diff <= 0, 0.0, masked)
    bias_mid = jnp.concatenate([bias_prev, bias_cur], axis=0).astype(BF16)
    bias_first = jnp.concatenate([jnp.where(t == 0, masked, bias_prev), bias_cur], axis=0).astype(BF16)
    r16 = lax.broadcasted_iota(jnp.int32, (2 * SUBLANES, 4 * WINDOW), 0)
    l16 = lax.broadcasted_iota(jnp.int32, (2 * SUBLANES, 4 * WINDOW), 1)
    ones_rows = jnp.where((r16 < SUBLANES) == (l16 < 2 * WINDOW), 1.0, 0.0).astype(BF16)
    zeros_v = jnp.zeros((HEAD_DIM, 2 * WINDOW), BF16)

    out_chunks = [[None] * nq for _ in range(D_MODEL // LANES)]
    for qb in range(nq):
        a0 = qb * WINDOW
        bias = bias_first if qb == 0 else bias_mid
        for kv in range(N_KV_HEADS):
            w_ext = jnp.concatenate(
                [jnp.concatenate([kx[2 * kv + par, a0:a0 + 2 * WINDOW, :], bias], axis=1) for par in range(2)],
                axis=0)
            q_ext = jnp.concatenate(
                [jnp.concatenate([qr[kv * PAIRS_PER_KV + j][a0:a0 + WINDOW, :], eye], axis=1)
                 for j in range(PAIRS_PER_KV)], axis=0)
            st = lax.dot_general(w_ext, q_ext, (((1,), (1,)), ((), ())), preferred_element_type=F32)
            p_rows = []
            sink_terms = []
            for par in range(2):
                blocks = []
                terms = []
                for j in range(PAIRS_PER_KV):
                    sink = sink_ref[kv * GROUP + 2 * j + par] * LOG2E
                    blk_s = st[par * 2 * WINDOW:(par + 1) * 2 * WINDOW, j * WINDOW:(j + 1) * WINDOW]
                    m = jnp.maximum(jnp.max(blk_s, axis=0, keepdims=True), sink)
                    blocks.append(jnp.exp2(blk_s - m).astype(BF16))
                    terms.append(jnp.exp2(sink - m))
                p_rows.append(jnp.concatenate(blocks, axis=1))
                sink_terms.append(jnp.concatenate(terms, axis=1))
            p_t = jnp.concatenate(p_rows, axis=0)
            v_win = vtx[kv * HEAD_DIM:(kv + 1) * HEAD_DIM, a0:a0 + 2 * WINDOW]
            vt_cat = jnp.concatenate([jnp.concatenate([v_win, zeros_v], axis=1),
                                      jnp.concatenate([zeros_v, v_win], axis=1),
                                      ones_rows], axis=0)
            o_t = _dot(vt_cat, p_t)
            den_e = o_t[2 * HEAD_DIM:2 * HEAD_DIM + SUBLANES, :] + sink_terms[0]
            den_o = o_t[2 * HEAD_DIM + SUBLANES:, :] + sink_terms[1]
            inv_e = jnp.concatenate([1.0 / den_e] * (HEAD_DIM // SUBLANES), axis=0)
            inv_o = jnp.concatenate([1.0 / den_o] * (HEAD_DIM // SUBLANES), axis=0)
            n_t = o_t[:2 * HEAD_DIM, :] * jnp.concatenate([inv_e, inv_o], axis=0)
            for j in range(PAIRS_PER_KV):
                out_chunks[kv * PAIRS_PER_KV + j][qb] = n_t[:, j * WINDOW:(j + 1) * WINDOW].T

    kx[:, 0:WINDOW, :] = kx[:, tm:tm + WINDOW, :]
    vtx[:, 0:WINDOW] = vtx[:, tm:tm + WINDOW]

    attn = jnp.concatenate([jnp.concatenate(col, axis=0) for col in out_chunks], axis=1)
    gated = (attn * _silu(z)).astype(BF16)
    h2 = h + _dot(gated, w_out_ref[...]) + b_out_ref[...]
    o_ref[0] = _rms_norm(h2, gf_ref[...])


def _odd_layer(h, positions, g, w_in, b_in, sinks, w_out, b_out, g_final, *, tm):
    b, s, d = h.shape
    tile = pl.BlockSpec((1, tm, d), lambda i, j: (i, j, 0))
    pos_spec = pl.BlockSpec((1, 1, tm), lambda i, j: (i, 0, j))
    weight_bytes = 2 * (w_in.size + w_out.size)
    tile_bytes = 4 * tm * d
    vmem_limit = weight_bytes + (4 + 12) * tile_bytes + (8 << 20)
    return pl.pallas_call(
        functools.partial(_odd_kernel, tm=tm),
        grid=(b, s // tm),
        in_specs=[pl.BlockSpec(memory_space=pltpu.SMEM), tile, pos_spec, _const_spec((1, d)),
                  _const_spec(w_in.shape), _const_spec((1, ODD_IN)), _const_spec(w_out.shape),
                  _const_spec((1, d)), _const_spec((1, d))],
        out_specs=tile,
        out_shape=jax.ShapeDtypeStruct(h.shape, h.dtype),
        scratch_shapes=[pltpu.VMEM((2 * N_KV_HEADS, WINDOW + tm, LANES), BF16),
                        pltpu.VMEM((KV_WIDTH, WINDOW + tm), BF16)],
        compiler_params=pltpu.CompilerParams(
            dimension_semantics=("arbitrary", "arbitrary"), vmem_limit_bytes=vmem_limit),
        name="odd_layer",
    )(sinks, h, positions.reshape(b, 1, s), g.reshape(1, d), w_in, b_in.reshape(1, ODD_IN), w_out,
      b_out.reshape(1, d), g_final.reshape(1, d))


EVEN_TM = 256
ODD_TM = 256


def kernel(x, positions, norm_g, w_in_even, w_pool, pool_scale, conv_w, w_out_even, w_in_odd, b_in_odd,
           attn_sinks, w_out_odd, b_out_odd, final_norm_g):
    assert norm_g.shape[0] == 2 and w_in_even.shape[0] == 1 and w_in_odd.shape[0] == 1
    h = _even_layer(x, norm_g[0], w_in_even[0].astype(BF16), w_pool[0].astype(BF16), pool_scale[0],
                    conv_w[0], w_out_even[0].astype(BF16), tm=EVEN_TM)
    return _odd_layer(h, positions, norm_g[1], w_in_odd[0].astype(BF16), b_in_odd[0], attn_sinks[0],
                      w_out_odd[0].astype(BF16), b_out_odd[0], final_norm_g, tm=ODD_TM)
```

```python
import functools
import math

import jax
import jax.numpy as jnp
from jax import lax
from jax.experimental import pallas as pl
from jax.experimental.pallas import tpu as pltpu

D_MODEL = 1024
NORM_EPS = 1e-5
POOL_WINDOWS = (2, 4, 8, 16)
POOL_GC = D_MODEL // len(POOL_WINDOWS)
CONV_WIDTH = D_MODEL
EVEN_IN = 6 * D_MODEL
HEAD_DIM = 64
N_HEADS = 16
N_KV_HEADS = 2
GROUP = N_HEADS // N_KV_HEADS
KV_WIDTH = N_KV_HEADS * HEAD_DIM
ODD_IN = 2 * D_MODEL + 2 * KV_WIDTH
WINDOW = 128
ROPE_THETA = 500000.0
ROT_DIMS = HEAD_DIM // 4
N_FREQ = ROT_DIMS // 2

LANES = 128
SUBLANES = 8
POOL_CARRY = 32
CONV_CARRY = SUBLANES
PAIRS_PER_KV = GROUP // 2
LOG2E = math.log2(math.e)
Q_SCALE = HEAD_DIM ** -0.5 * LOG2E
MASK_BIAS = 2.0 ** 100

_INV_FREQ = tuple(float(ROPE_THETA ** (-(2.0 * i) / ROT_DIMS)) for i in range(N_FREQ))

F32 = jnp.float32
BF16 = jnp.bfloat16


def _dot(a, b):
    return jnp.dot(a, b, preferred_element_type=F32)


def _rms_norm(x, g):
    ms = jnp.mean(x * x, axis=-1, keepdims=True)
    return x * lax.rsqrt(ms + NORM_EPS) * g


def _silu(z):
    return z * jax.nn.sigmoid(z)


def _rope_tables(pos_row, tm):
    sub = lax.broadcasted_iota(jnp.int32, (N_FREQ, tm), 0)
    inv_freq = jnp.full((N_FREQ, tm), _INV_FREQ[N_FREQ - 1], F32)
    for f in range(N_FREQ - 1):
        inv_freq = jnp.where(sub == f, _INV_FREQ[f], inv_freq)
    ang = pos_row.astype(F32) * inv_freq
    cos8 = jnp.cos(ang)
    sin8 = jnp.sin(ang)
    r = lax.broadcasted_iota(jnp.int32, (LANES, LANES), 0) & (HEAD_DIM - 1)
    c_blocks, s_blocks = [], []
    for blk in range(tm // LANES):
        c_rep = jnp.concatenate([cos8[:, blk * LANES:(blk + 1) * LANES]] * (LANES // N_FREQ), axis=0)
        s_rep = jnp.concatenate([sin8[:, blk * LANES:(blk + 1) * LANES]] * (LANES // N_FREQ), axis=0)
        c_pat = jnp.where(r < ROT_DIMS, c_rep, 1.0)
        s_pat = jnp.where(r < N_FREQ, -s_rep, jnp.where(r < ROT_DIMS, s_rep, 0.0))
        c_blocks.append(c_pat.T)
        s_blocks.append(s_pat.T)
    return jnp.concatenate(c_blocks, axis=0), jnp.concatenate(s_blocks, axis=0)


def _even_kernel(x_ref, g_ref, w_in_ref, w_pool_ref, ps_ref, conv_ref, w_out_ref, o_ref,
                 ubuf, s2buf, s4buf, cbuf, *, tm):
    t = pl.program_id(1)
    n_slabs = D_MODEL // LANES
    slabs_per_group = POOL_GC // LANES

    @pl.when(t == 0)
    def _():
        ubuf[:, 0:POOL_CARRY, :] = jnp.zeros((n_slabs, POOL_CARRY, LANES), F32)
        cbuf[:, 0:CONV_CARRY, :] = jnp.zeros((n_slabs, CONV_CARRY, LANES), F32)

    x = x_ref[0]
    y = _rms_norm(x, g_ref[...]).astype(BF16)

    u = _dot(y, w_in_ref[:, 0:D_MODEL])
    gate_c = _dot(y, w_in_ref[:, 2 * D_MODEL:3 * D_MODEL])
    h_c = _dot(y, w_in_ref[:, 3 * D_MODEL:4 * D_MODEL])
    gate_b = _dot(y, w_in_ref[:, D_MODEL:2 * D_MODEL])
    z_b = _dot(y, w_in_ref[:, 5 * D_MODEL:6 * D_MODEL])
    z_a = _dot(y, w_in_ref[:, 4 * D_MODEL:5 * D_MODEL])

    ext = POOL_CARRY + tm
    pos = t * tm + lax.broadcasted_iota(jnp.int32, (tm, LANES), 0)
    a_parts = []
    for g, w in enumerate(POOL_WINDOWS):
        inv = 1.0 / jnp.minimum(pos + 1, w).astype(F32)
        pooled = []
        for j in range(g * slabs_per_group, (g + 1) * slabs_per_group):
            u_j = u[:, j * LANES:(j + 1) * LANES]
            ubuf[j, POOL_CARRY:, :] = u_j
            s = ubuf[j, 8:ext, :] + ubuf[j, 7:ext - 1, :]
            if w >= 4:
                jj = j - slabs_per_group
                s2buf[jj, :, :] = s
                s = s[8:, :] + s2buf[jj, 6:ext - 10, :]
            if w >= 8:
                jj = j - 2 * slabs_per_group
                s4buf[jj, :, :] = s
                s = s[8:, :] + s4buf[jj, 4:ext - 20, :]
            if w >= 16:
                s = s[8:, :] + s[:-8, :]
            pooled.append(s[s.shape[0] - tm:, :] * inv - u_j)
        pooled = jnp.concatenate(pooled, axis=1)
        a_parts.append(_dot(pooled.astype(BF16), w_pool_ref[g]))
    a = jnp.concatenate(a_parts, axis=1) * ps_ref[...]

    cu = gate_c * h_c
    cw = conv_ref[...]
    v_parts = []
    for j in range(n_slabs):
        lanes = slice(j * LANES, (j + 1) * LANES)
        cbuf[j, CONV_CARRY:, :] = cu[:, lanes]
        v_parts.append(cw[2:3, lanes] * cu[:, lanes]
                       + cw[1:2, lanes] * cbuf[j, CONV_CARRY - 1:CONV_CARRY - 1 + tm, :]
                       + cw[0:1, lanes] * cbuf[j, CONV_CARRY - 2:CONV_CARRY - 2 + tm, :])
    bo = gate_b * jnp.concatenate(v_parts, axis=1)

    ubuf[:, 0:POOL_CARRY, :] = ubuf[:, tm:tm + POOL_CARRY, :]
    cbuf[:, 0:CONV_CARRY, :] = cbuf[:, tm:tm + CONV_CARRY, :]

    mix_b = (bo * _silu(z_b)).astype(BF16)
    mix_a = (a * _silu(z_a)).astype(BF16)
    o_ref[0] = x + (_dot(mix_b, w_out_ref[D_MODEL:, 0:D_MODEL]) + _dot(mix_a, w_out_ref[0:D_MODEL, 0:D_MODEL]))


def _const_spec(shape):
    nd = len(shape)
    return pl.BlockSpec(shape, lambda *_: (0,) * nd, pipeline_mode=pl.Buffered(1))


def _even_layer(x, g, w_in, w_pool, pool_scale, conv_w, w_out, *, tm):
    b, s, d = x.shape
    tile = pl.BlockSpec((1, tm, d), lambda i, j: (i, j, 0))
    n_slabs, per_group = d // LANES, POOL_GC // LANES
    weight_bytes = 2 * (w_in.size + w_pool.size + w_out.size)
    tile_bytes = 4 * tm * d
    vmem_limit = weight_bytes + (4 + 3 + 10) * tile_bytes + (4 << 20)
    return pl.pallas_call(
        functools.partial(_even_kernel, tm=tm),
        grid=(b, s // tm),
        in_specs=[tile, _const_spec((1, d)), _const_spec(w_in.shape), _const_spec(w_pool.shape),
                  _const_spec((1, d)), _const_spec(conv_w.shape), _const_spec(w_out.shape)],
        out_specs=tile,
        out_shape=jax.ShapeDtypeStruct(x.shape, x.dtype),
        scratch_shapes=[pltpu.VMEM((n_slabs, POOL_CARRY + tm, LANES), F32),
                        pltpu.VMEM((n_slabs - per_group, POOL_CARRY + tm - 8, LANES), F32),
                        pltpu.VMEM((n_slabs - 2 * per_group, POOL_CARRY + tm - 16, LANES), F32),
                        pltpu.VMEM((n_slabs, CONV_CARRY + tm, LANES), F32)],
        compiler_params=pltpu.CompilerParams(
            dimension_semantics=("arbitrary", "arbitrary"), vmem_limit_bytes=vmem_limit),
        name="even_layer",
    )(x, g.reshape(1, d), w_in, w_pool, pool_scale.reshape(1, d), conv_w, w_out)


def _odd_kernel(sink_ref, h_ref, pos_ref, g_ref, w_in_ref, b_in_ref, w_out_ref, b_out_ref, gf_ref,
                o_ref, kx, vtx, *, tm):
    t = pl.program_id(1)
    nq = tm // WINDOW

    @pl.when(t == 0)
    def _():
        kx[:, 0:WINDOW, :] = jnp.zeros((2 * N_KV_HEADS, WINDOW, LANES), BF16)
        vtx[:, 0:WINDOW] = jnp.zeros((KV_WIDTH, WINDOW), BF16)

    h = h_ref[0]
    y = _rms_norm(h, g_ref[...]).astype(BF16)
    proj = _dot(y, w_in_ref[...]) + b_in_ref[...]
    k = proj[:, D_MODEL:D_MODEL + KV_WIDTH]
    v = proj[:, D_MODEL + KV_WIDTH:D_MODEL + 2 * KV_WIDTH]
    z = proj[:, D_MODEL + 2 * KV_WIDTH:]

    c_tab, s_tab = _rope_tables(pos_ref[0], tm)
    lane = lax.broadcasted_iota(jnp.int32, (tm, LANES), 1)
    first_half = (lane & (HEAD_DIM - 1)) < N_FREQ

    def rope(xc, cc, sc):
        swapped = jnp.where(first_half, pltpu.roll(xc, LANES - N_FREQ, 1), pltpu.roll(xc, N_FREQ, 1))
        return xc * cc + swapped * sc

    cq = c_tab * Q_SCALE
    sq = s_tab * Q_SCALE
    qr = [rope(proj[:, j * LANES:(j + 1) * LANES], cq, sq).astype(BF16) for j in range(D_MODEL // LANES)]
    kr = rope(k, c_tab, s_tab)

    head0 = lane < HEAD_DIM
    lo0 = jnp.where(head0, kr, 0.0)
    hi1 = jnp.where(head0, 0.0, kr)
    kx[0, WINDOW:, :] = lo0.astype(BF16)
    kx[1, WINDOW:, :] = pltpu.roll(lo0, HEAD_DIM, 1).astype(BF16)
    kx[2, WINDOW:, :] = pltpu.roll(hi1, HEAD_DIM, 1).astype(BF16)
    kx[3, WINDOW:, :] = hi1.astype(BF16)
    for blk in range(nq):
        vtx[:, WINDOW + blk * WINDOW:WINDOW + (blk + 1) * WINDOW] = (
            v[blk * WINDOW:(blk + 1) * WINDOW, :].T.astype(BF16))

    slot = lax.broadcasted_iota(jnp.int32, (WINDOW, LANES), 0)
    qidx = lax.broadcasted_iota(jnp.int32, (WINDOW, LANES), 1)
    diff = slot - qidx
    eye = jnp.where(diff == 0, 1.0, 0.0).astype(BF16)
    masked = jnp.full((WINDOW, LANES), -MASK_BIAS, F32)
    bias_prev = jnp.where(diff > 0, 0.0, masked)
    bias_cur = jnp.where(diff <= 0, 0.0, masked)
    bias_mid = jnp.concatenate([bias_prev, bias_cur], axis=0).astype(BF16)
    bias_first = jnp.concatenate([jnp.where(t == 0, masked, bias_prev), bias_cur], axis=0).astype(BF16)
    r16 = lax.broadcasted_iota(jnp.int32, (2 * SUBLANES, 4 * WINDOW), 0)
    l16 = lax.broadcasted_iota(jnp.int32, (2 * SUBLANES, 4 * WINDOW), 1)
    ones_rows = jnp.where((r16 < SUBLANES) == (l16 < 2 * WINDOW), 1.0, 0.0).astype(BF16)
    zeros_v = jnp.zeros((HEAD_DIM, 2 * WINDOW), BF16)

    out_chunks = [[None] * nq for _ in range(D_MODEL // LANES)]
    for qb in range(nq):
        a0 = qb * WINDOW
        bias = bias_first if qb == 0 else bias_mid
        for kv in range(N_KV_HEADS):
            w_ext = jnp.concatenate(
                [jnp.concatenate([kx[2 * kv + par, a0:a0 + 2 * WINDOW, :], bias], axis=1) for par in range(2)],
                axis=0)
            q_ext = jnp.concatenate(
                [jnp.concatenate([qr[kv * PAIRS_PER_KV + j][a0:a0 + WINDOW, :], eye], axis=1)
                 for j in range(PAIRS_PER_KV)], axis=0)
            st = lax.dot_general(w_ext, q_ext, (((1,), (1,)), ((), ())), preferred_element_type=F32)
            p_rows = []
            sink_terms = []
            for par in range(2):
                blocks = []
                terms = []
                for j in range(PAIRS_PER_KV):
                    sink = sink_ref[kv * GROUP + 2 * j + par] * LOG2E
                    blk_s = st[par * 2 * WINDOW:(par + 1) * 2 * WINDOW, j * WINDOW:(j + 1) * WINDOW]
                    m = jnp.maximum(jnp.max(blk_s, axis=0, keepdims=True), sink)
                    blocks.append(jnp.exp2(blk_s - m).astype(BF16))
                    terms.append(jnp.exp2(sink - m))
                p_rows.append(jnp.concatenate(blocks, axis=1))
                sink_terms.append(jnp.concatenate(terms, axis=1))
            p_t = jnp.concatenate(p_rows, axis=0)
            v_win = vtx[kv * HEAD_DIM:(kv + 1) * HEAD_DIM, a0:a0 + 2 * WINDOW]
            vt_cat = jnp.concatenate([jnp.concatenate([v_win, zeros_v], axis=1),
                                      jnp.concatenate([zeros_v, v_win], axis=1),
                                      ones_rows], axis=0)
            o_t = _dot(vt_cat, p_t)
            den_e = o_t[2 * HEAD_DIM:2 * HEAD_DIM + SUBLANES, :] + sink_terms[0]
            den_o = o_t[2 * HEAD_DIM + SUBLANES:, :] + sink_terms[1]
            inv_e = jnp.concatenate([1.0 / den_e] * (HEAD_DIM // SUBLANES), axis=0)
            inv_o = jnp.concatenate([1.0 / den_o] * (HEAD_DIM // SUBLANES), axis=0)
            n_t = o_t[:2 * HEAD_DIM, :] * jnp.concatenate([inv_e, inv_o], axis=0)
            for j in range(PAIRS_PER_KV):
                out_chunks[kv * PAIRS_PER_KV + j][qb] = n_t[:, j * WINDOW:(j + 1) * WINDOW].T

    kx[:, 0:WINDOW, :] = kx[:, tm:tm + WINDOW, :]
    vtx[:, 0:WINDOW] = vtx[:, tm:tm + WINDOW]

    attn = jnp.concatenate([jnp.concatenate(col, axis=0) for col in out_chunks], axis=1)
    gated = (attn * _silu(z)).astype(BF16)
    h2 = h + _dot(gated, w_out_ref[:, 0:D_MODEL]) + b_out_ref[...]
    o_ref[0] = _rms_norm(h2, gf_ref[...])


def _odd_layer(h, positions, g, w_in, b_in, sinks, w_out, b_out, g_final, *, tm):
    b, s, d = h.shape
    tile = pl.BlockSpec((1, tm, d), lambda i, j: (i, j, 0))
    pos_spec = pl.BlockSpec((1, 1, tm), lambda i, j: (i, 0, j))
    weight_bytes = 2 * (w_in.size + w_out.size)
    tile_bytes = 4 * tm * d
    vmem_limit = weight_bytes + (4 + 12) * tile_bytes + (8 << 20)
    return pl.pallas_call(
        functools.partial(_odd_kernel, tm=tm),
        grid=(b, s // tm),
        in_specs=[pl.BlockSpec(memory_space=pltpu.SMEM), tile, pos_spec, _const_spec((1, d)),
                  _const_spec(w_in.shape), _const_spec((1, ODD_IN)), _const_spec(w_out.shape),
                  _const_spec((1, d)), _const_spec((1, d))],
        out_specs=tile,
        out_shape=jax.ShapeDtypeStruct(h.shape, h.dtype),
        scratch_shapes=[pltpu.VMEM((2 * N_KV_HEADS, WINDOW + tm, LANES), BF16),
                        pltpu.VMEM((KV_WIDTH, WINDOW + tm), BF16)],
        compiler_params=pltpu.CompilerParams(
            dimension_semantics=("arbitrary", "arbitrary"), vmem_limit_bytes=vmem_limit),
        name="odd_layer",
    )(sinks, h, positions.reshape(b, 1, s), g.reshape(1, d), w_in, b_in.reshape(1, ODD_IN), w_out,
      b_out.reshape(1, d), g_final.reshape(1, d))


EVEN_TM = 512
ODD_TM = 256


def _mxu_weight(w):
    w = w.astype(BF16)
    if (w.shape[-1] // LANES) % 2 == 0:
        w = jnp.pad(w, [(0, 0)] * (w.ndim - 1) + [(0, LANES)])
    return w


def kernel(x, positions, norm_g, w_in_even, w_pool, pool_scale, conv_w, w_out_even, w_in_odd, b_in_odd,
           attn_sinks, w_out_odd, b_out_odd, final_norm_g):
    assert norm_g.shape[0] == 2 and w_in_even.shape[0] == 1 and w_in_odd.shape[0] == 1
    h = _even_layer(x, norm_g[0], _mxu_weight(w_in_even[0]), w_pool[0].astype(BF16), pool_scale[0],
                    conv_w[0], _mxu_weight(w_out_even[0]), tm=EVEN_TM)
    return _odd_layer(h, positions, norm_g[1], w_in_odd[0].astype(BF16), b_in_odd[0], attn_sinks[0],
                      _mxu_weight(w_out_odd[0]), b_out_odd[0], final_norm_g, tm=ODD_TM)
```

```python
import functools
import math

import jax
import jax.numpy as jnp
from jax import lax
from jax.experimental import pallas as pl
from jax.experimental.pallas import tpu as pltpu

D_MODEL = 1024
NORM_EPS = 1e-5
POOL_WINDOWS = (2, 4, 8, 16)
POOL_GC = D_MODEL // len(POOL_WINDOWS)
CONV_WIDTH = D_MODEL
EVEN_IN = 6 * D_MODEL
HEAD_DIM = 64
N_HEADS = 16
N_KV_HEADS = 2
GROUP = N_HEADS // N_KV_HEADS
KV_WIDTH = N_KV_HEADS * HEAD_DIM
ODD_IN = 2 * D_MODEL + 2 * KV_WIDTH
WINDOW = 128
ROPE_THETA = 500000.0
ROT_DIMS = HEAD_DIM // 4
N_FREQ = ROT_DIMS // 2

LANES = 128
SUBLANES = 8
POOL_CARRY = 32
CONV_CARRY = SUBLANES
PAIRS_PER_KV = GROUP // 2
LOG2E = math.log2(math.e)
Q_SCALE = HEAD_DIM ** -0.5 * LOG2E
MASK_BIAS = 2.0 ** 100

_INV_FREQ = tuple(float(ROPE_THETA ** (-(2.0 * i) / ROT_DIMS)) for i in range(N_FREQ))

F32 = jnp.float32
BF16 = jnp.bfloat16


def _dot(a, b):
    return jnp.dot(a, b, preferred_element_type=F32)


def _rms_norm(x, g):
    ms = jnp.mean(x * x, axis=-1, keepdims=True)
    return x * lax.rsqrt(ms + NORM_EPS) * g


def _silu(z):
    return z * jax.nn.sigmoid(z)


def _rope_tables(pos_row, tm):
    sub = lax.broadcasted_iota(jnp.int32, (N_FREQ, tm), 0)
    inv_freq = jnp.full((N_FREQ, tm), _INV_FREQ[N_FREQ - 1], F32)
    for f in range(N_FREQ - 1):
        inv_freq = jnp.where(sub == f, _INV_FREQ[f], inv_freq)
    ang = pos_row.astype(F32) * inv_freq
    cos8 = jnp.cos(ang)
    sin8 = jnp.sin(ang)
    r = lax.broadcasted_iota(jnp.int32, (LANES, LANES), 0) & (HEAD_DIM - 1)
    c_blocks, s_blocks = [], []
    for blk in range(tm // LANES):
        c_rep = jnp.concatenate([cos8[:, blk * LANES:(blk + 1) * LANES]] * (LANES // N_FREQ), axis=0)
        s_rep = jnp.concatenate([sin8[:, blk * LANES:(blk + 1) * LANES]] * (LANES // N_FREQ), axis=0)
        c_pat = jnp.where(r < ROT_DIMS, c_rep, 1.0)
        s_pat = jnp.where(r < N_FREQ, -s_rep, jnp.where(r < ROT_DIMS, s_rep, 0.0))
        c_blocks.append(c_pat.T)
        s_blocks.append(s_pat.T)
    return jnp.concatenate(c_blocks, axis=0), jnp.concatenate(s_blocks, axis=0)


def _even_kernel(x_ref, g_ref, w_in_ref, w_pool_ref, ps_ref, conv_ref, w_out_ref, o_ref,
                 ubuf, s2buf, s4buf, cbuf, *, tm):
    t = pl.program_id(1)
    n_slabs = D_MODEL // LANES
    slabs_per_group = POOL_GC // LANES

    @pl.when(t == 0)
    def _():
        ubuf[:, 0:POOL_CARRY, :] = jnp.zeros((n_slabs, POOL_CARRY, LANES), F32)
        cbuf[:, 0:CONV_CARRY, :] = jnp.zeros((n_slabs, CONV_CARRY, LANES), F32)

    x = x_ref[0]
    y = _rms_norm(x, g_ref[...]).astype(BF16)

    u = _dot(y, w_in_ref[:, 0:D_MODEL])
    gate_c = _dot(y, w_in_ref[:, 2 * D_MODEL:3 * D_MODEL])
    h_c = _dot(y, w_in_ref[:, 3 * D_MODEL:4 * D_MODEL])
    gate_b = _dot(y, w_in_ref[:, D_MODEL:2 * D_MODEL])
    z_b = _dot(y, w_in_ref[:, 5 * D_MODEL:6 * D_MODEL])
    z_a = _dot(y, w_in_ref[:, 4 * D_MODEL:5 * D_MODEL])

    ext = POOL_CARRY + tm
    pos = t * tm + lax.broadcasted_iota(jnp.int32, (tm, LANES), 0)
    a_parts = []
    for g, w in enumerate(POOL_WINDOWS):
        inv = 1.0 / jnp.minimum(pos + 1, w).astype(F32)
        pooled = []
        for j in range(g * slabs_per_group, (g + 1) * slabs_per_group):
            u_j = u[:, j * LANES:(j + 1) * LANES]
            ubuf[j, POOL_CARRY:, :] = u_j
            s = ubuf[j, 8:ext, :] + ubuf[j, 7:ext - 1, :]
            if w >= 4:
                jj = j - slabs_per_group
                s2buf[jj, :, :] = s
                s = s[8:, :] + s2buf[jj, 6:ext - 10, :]
            if w >= 8:
                jj = j - 2 * slabs_per_group
                s4buf[jj, :, :] = s
                s = s[8:, :] + s4buf[jj, 4:ext - 20, :]
            if w >= 16:
                s = s[8:, :] + s[:-8, :]
            pooled.append(s[s.shape[0] - tm:, :] * inv - u_j)
        pooled = jnp.concatenate(pooled, axis=1)
        a_parts.append(_dot(pooled.astype(BF16), w_pool_ref[g]))
    a = jnp.concatenate(a_parts, axis=1) * ps_ref[...]

    cu = gate_c * h_c
    cw = conv_ref[...]
    v_parts = []
    for j in range(n_slabs):
        lanes = slice(j * LANES, (j + 1) * LANES)
        cbuf[j, CONV_CARRY:, :] = cu[:, lanes]
        v_parts.append(cw[2:3, lanes] * cu[:, lanes]
                       + cw[1:2, lanes] * cbuf[j, CONV_CARRY - 1:CONV_CARRY - 1 + tm, :]
                       + cw[0:1, lanes] * cbuf[j, CONV_CARRY - 2:CONV_CARRY - 2 + tm, :])
    bo = gate_b * jnp.concatenate(v_parts, axis=1)

    ubuf[:, 0:POOL_CARRY, :] = ubuf[:, tm:tm + POOL_CARRY, :]
    cbuf[:, 0:CONV_CARRY, :] = cbuf[:, tm:tm + CONV_CARRY, :]

    mix_b = (bo * _silu(z_b)).astype(BF16)
    mix_a = (a * _silu(z_a)).astype(BF16)
    o_ref[0] = x + (_dot(mix_b, w_out_ref[D_MODEL:, 0:D_MODEL]) + _dot(mix_a, w_out_ref[0:D_MODEL, 0:D_MODEL]))


def _const_spec(shape):
    nd = len(shape)
    return pl.BlockSpec(shape, lambda *_: (0,) * nd, pipeline_mode=pl.Buffered(1))


def _even_layer(x, g, w_in, w_pool, pool_scale, conv_w, w_out, *, tm):
    b, s, d = x.shape
    tile = pl.BlockSpec((1, tm, d), lambda i, j: (i, j, 0))
    n_slabs, per_group = d // LANES, POOL_GC // LANES
    weight_bytes = 2 * (w_in.size + w_pool.size + w_out.size)
    tile_bytes = 4 * tm * d
    vmem_limit = weight_bytes + (4 + 3 + 10) * tile_bytes + (4 << 20)
    return pl.pallas_call(
        functools.partial(_even_kernel, tm=tm),
        grid=(b, s // tm),
        in_specs=[tile, _const_spec((1, d)), _const_spec(w_in.shape), _const_spec(w_pool.shape),
                  _const_spec((1, d)), _const_spec(conv_w.shape), _const_spec(w_out.shape)],
        out_specs=tile,
        out_shape=jax.ShapeDtypeStruct(x.shape, x.dtype),
        scratch_shapes=[pltpu.VMEM((n_slabs, POOL_CARRY + tm, LANES), F32),
                        pltpu.VMEM((n_slabs - per_group, POOL_CARRY + tm - 8, LANES), F32),
                        pltpu.VMEM((n_slabs - 2 * per_group, POOL_CARRY + tm - 16, LANES), F32),
                        pltpu.VMEM((n_slabs, CONV_CARRY + tm, LANES), F32)],
        compiler_params=pltpu.CompilerParams(
            dimension_semantics=("arbitrary", "arbitrary"), vmem_limit_bytes=vmem_limit),
        name="even_layer",
    )(x, g.reshape(1, d), w_in, w_pool, pool_scale.reshape(1, d), conv_w, w_out)


def _odd_kernel(sink_ref, h_ref, pos_ref, g_ref, w_in_ref, b_in_ref, w_out_ref, b_out_ref, gf_ref,
                o_ref, kx, vtx, *, tm):
    t = pl.program_id(1)
    nq = tm // WINDOW

    @pl.when(t == 0)
    def _():
        kx[:, 0:WINDOW, :] = jnp.zeros((2 * N_KV_HEADS, WINDOW, LANES), BF16)
        vtx[:, 0:WINDOW] = jnp.zeros((KV_WIDTH, WINDOW), BF16)

    h = h_ref[0]
    y = _rms_norm(h, g_ref[...]).astype(BF16)
    qkv_w = D_MODEL + 2 * KV_WIDTH
    kv = _dot(y, w_in_ref[:, D_MODEL:qkv_w]) + b_in_ref[:, D_MODEL:qkv_w]
    proj = _dot(y, w_in_ref[:, 0:D_MODEL]) + b_in_ref[:, 0:D_MODEL]
    k = kv[:, 0:KV_WIDTH]
    v = kv[:, KV_WIDTH:]

    c_tab, s_tab = _rope_tables(pos_ref[0], tm)
    lane = lax.broadcasted_iota(jnp.int32, (tm, LANES), 1)
    first_half = (lane & (HEAD_DIM - 1)) < N_FREQ

    def rope(xc, cc, sc):
        swapped = jnp.where(first_half, pltpu.roll(xc, LANES - N_FREQ, 1), pltpu.roll(xc, N_FREQ, 1))
        return xc * cc + swapped * sc

    cq = c_tab * Q_SCALE
    sq = s_tab * Q_SCALE
    qr = [rope(proj[:, j * LANES:(j + 1) * LANES], cq, sq).astype(BF16) for j in range(D_MODEL // LANES)]
    kr = rope(k, c_tab, s_tab)

    head0 = lane < HEAD_DIM
    lo0 = jnp.where(head0, kr, 0.0)
    hi1 = jnp.where(head0, 0.0, kr)
    kx[0, WINDOW:, :] = lo0.astype(BF16)
    kx[1, WINDOW:, :] = pltpu.roll(lo0, HEAD_DIM, 1).astype(BF16)
    kx[2, WINDOW:, :] = pltpu.roll(hi1, HEAD_DIM, 1).astype(BF16)
    kx[3, WINDOW:, :] = hi1.astype(BF16)
    for blk in range(nq):
        vtx[:, WINDOW + blk * WINDOW:WINDOW + (blk + 1) * WINDOW] = (
            v[blk * WINDOW:(blk + 1) * WINDOW, :].T.astype(BF16))

    slot = lax.broadcasted_iota(jnp.int32, (WINDOW, LANES), 0)
    qidx = lax.broadcasted_iota(jnp.int32, (WINDOW, LANES), 1)
    diff = slot - qidx
    eye = jnp.where(diff == 0, 1.0, 0.0).astype(BF16)
    masked = jnp.full((WINDOW, LANES), -MASK_BIAS, F32)
    bias_prev = jnp.where(diff > 0, 0.0, masked)
    bias_cur = jnp.where(diff <= 0, 0.0, masked)
    bias_mid = jnp.concatenate([bias_prev, bias_cur], axis=0).astype(BF16)
    bias_first = jnp.concatenate([jnp.where(t == 0, masked, bias_prev), bias_cur], axis=0).astype(BF16)
    r16 = lax.broadcasted_iota(jnp.int32, (2 * SUBLANES, 4 * WINDOW), 0)
    l16 = lax.broadcasted_iota(jnp.int32, (2 * SUBLANES, 4 * WINDOW), 1)
    ones_rows = jnp.where((r16 < SUBLANES) == (l16 < 2 * WINDOW), 1.0, 0.0).astype(BF16)
    zeros_v = jnp.zeros((HEAD_DIM, 2 * WINDOW), BF16)

    out_chunks = [[None] * nq for _ in range(D_MODEL // LANES)]
    units = [(qb, kv) for qb in range(nq) for kv in range(N_KV_HEADS)]

    def scores(qb, kv):
        a0 = qb * WINDOW
        bias = bias_first if qb == 0 else bias_mid
        w_ext = jnp.concatenate(
            [jnp.concatenate([kx[2 * kv + par, a0:a0 + 2 * WINDOW, :], bias], axis=1) for par in range(2)],
            axis=0)
        q_ext = jnp.concatenate(
            [jnp.concatenate([qr[kv * PAIRS_PER_KV + j][a0:a0 + WINDOW, :], eye], axis=1)
             for j in range(PAIRS_PER_KV)], axis=0)
        return lax.dot_general(w_ext, q_ext, (((1,), (1,)), ((), ())), preferred_element_type=F32)

    def attend(qb, kv, st):
        a0 = qb * WINDOW
        p_rows = []
        sink_terms = []
        for par in range(2):
            blocks = []
            terms = []
            for j in range(PAIRS_PER_KV):
                sink = sink_ref[kv * GROUP + 2 * j + par] * LOG2E
                blk_s = st[par * 2 * WINDOW:(par + 1) * 2 * WINDOW, j * WINDOW:(j + 1) * WINDOW]
                m = jnp.maximum(jnp.max(blk_s, axis=0, keepdims=True), sink)
                blocks.append(jnp.exp2(blk_s - m).astype(BF16))
                terms.append(jnp.exp2(sink - m))
            p_rows.append(jnp.concatenate(blocks, axis=1))
            sink_terms.append(jnp.concatenate(terms, axis=1))
        p_t = jnp.concatenate(p_rows, axis=0)
        v_win = vtx[kv * HEAD_DIM:(kv + 1) * HEAD_DIM, a0:a0 + 2 * WINDOW]
        vt_cat = jnp.concatenate([jnp.concatenate([v_win, zeros_v], axis=1),
                                  jnp.concatenate([zeros_v, v_win], axis=1),
                                  ones_rows], axis=0)
        o_t = _dot(vt_cat, p_t)
        den_e = o_t[2 * HEAD_DIM:2 * HEAD_DIM + SUBLANES, :] + sink_terms[0]
        den_o = o_t[2 * HEAD_DIM + SUBLANES:, :] + sink_terms[1]
        inv_e = jnp.concatenate([1.0 / den_e] * (HEAD_DIM // SUBLANES), axis=0)
        inv_o = jnp.concatenate([1.0 / den_o] * (HEAD_DIM // SUBLANES), axis=0)
        n_t = o_t[:2 * HEAD_DIM, :] * jnp.concatenate([inv_e, inv_o], axis=0)
        for j in range(PAIRS_PER_KV):
            out_chunks[kv * PAIRS_PER_KV + j][qb] = n_t[:, j * WINDOW:(j + 1) * WINDOW].T

    z_cols = 2 * LANES
    n_zc = D_MODEL // z_cols
    z_rows = tm * n_zc // len(units)
    z_parts = [[None] * n_zc for _ in range(tm // z_rows)]

    def gate_piece(i):
        r0, c0 = (i // n_zc) * z_rows, qkv_w + (i % n_zc) * z_cols
        z_parts[i // n_zc][i % n_zc] = (_dot(y[r0:r0 + z_rows, :], w_in_ref[:, c0:c0 + z_cols])
                                        + b_in_ref[:, c0:c0 + z_cols])

    gate_piece(0)
    st_next = scores(*units[0])
    for i, unit in enumerate(units):
        st = st_next
        if i + 1 < len(units):
            gate_piece(i + 1)
            st_next = scores(*units[i + 1])
        attend(*unit, st)

    kx[:, 0:WINDOW, :] = kx[:, tm:tm + WINDOW, :]
    vtx[:, 0:WINDOW] = vtx[:, tm:tm + WINDOW]

    z = jnp.concatenate([jnp.concatenate(row, axis=1) for row in z_parts], axis=0)
    attn = jnp.concatenate([jnp.concatenate(col, axis=0) for col in out_chunks], axis=1)
    gated = (attn * _silu(z)).astype(BF16)
    h2 = h + _dot(gated, w_out_ref[:, 0:D_MODEL]) + b_out_ref[...]
    o_ref[0] = _rms_norm(h2, gf_ref[...])


def _odd_layer(h, positions, g, w_in, b_in, sinks, w_out, b_out, g_final, *, tm):
    b, s, d = h.shape
    tile = pl.BlockSpec((1, tm, d), lambda i, j: (i, j, 0))
    pos_spec = pl.BlockSpec((1, 1, tm), lambda i, j: (i, 0, j))
    weight_bytes = 2 * (w_in.size + w_out.size)
    tile_bytes = 4 * tm * d
    vmem_limit = weight_bytes + (4 + 12) * tile_bytes + (8 << 20)
    return pl.pallas_call(
        functools.partial(_odd_kernel, tm=tm),
        grid=(b, s // tm),
        in_specs=[pl.BlockSpec(memory_space=pltpu.SMEM), tile, pos_spec, _const_spec((1, d)),
                  _const_spec(w_in.shape), _const_spec((1, ODD_IN)), _const_spec(w_out.shape),
                  _const_spec((1, d)), _const_spec((1, d))],
        out_specs=tile,
        out_shape=jax.ShapeDtypeStruct(h.shape, h.dtype),
        scratch_shapes=[pltpu.VMEM((2 * N_KV_HEADS, WINDOW + tm, LANES), BF16),
                        pltpu.VMEM((KV_WIDTH, WINDOW + tm), BF16)],
        compiler_params=pltpu.CompilerParams(
            dimension_semantics=("arbitrary", "arbitrary"), vmem_limit_bytes=vmem_limit),
        name="odd_layer",
    )(sinks, h, positions.reshape(b, 1, s), g.reshape(1, d), w_in, b_in.reshape(1, ODD_IN), w_out,
      b_out.reshape(1, d), g_final.reshape(1, d))


EVEN_TM = 512
ODD_TM = 512


def _mxu_weight(w):
    w = w.astype(BF16)
    if (w.shape[-1] // LANES) % 2 == 0:
        w = jnp.pad(w, [(0, 0)] * (w.ndim - 1) + [(0, LANES)])
    return w


def kernel(x, positions, norm_g, w_in_even, w_pool, pool_scale, conv_w, w_out_even, w_in_odd, b_in_odd,
           attn_sinks, w_out_odd, b_out_odd, final_norm_g):
    assert norm_g.shape[0] == 2 and w_in_even.shape[0] == 1 and w_in_odd.shape[0] == 1
    h = _even_layer(x, norm_g[0], _mxu_weight(w_in_even[0]), w_pool[0].astype(BF16), pool_scale[0],
                    conv_w[0], _mxu_weight(w_out_even[0]), tm=EVEN_TM)
    return _odd_layer(h, positions, norm_g[1], w_in_odd[0].astype(BF16), b_in_odd[0], attn_sinks[0],
                      _mxu_weight(w_out_odd[0]), b_out_odd[0], final_norm_g, tm=ODD_TM)
```

```python
import functools
import math

import jax
import jax.numpy as jnp
from jax import lax
from jax.experimental import pallas as pl
from jax.experimental.pallas import tpu as pltpu

D_MODEL = 1024
NORM_EPS = 1e-5
POOL_WINDOWS = (2, 4, 8, 16)
POOL_GC = D_MODEL // len(POOL_WINDOWS)
CONV_WIDTH = D_MODEL
EVEN_IN = 6 * D_MODEL
HEAD_DIM = 64
N_HEADS = 16
N_KV_HEADS = 2
GROUP = N_HEADS // N_KV_HEADS
KV_WIDTH = N_KV_HEADS * HEAD_DIM
ODD_IN = 2 * D_MODEL + 2 * KV_WIDTH
WINDOW = 128
ROPE_THETA = 500000.0
ROT_DIMS = HEAD_DIM // 4
N_FREQ = ROT_DIMS // 2

LANES = 128
SUBLANES = 8
POOL_CARRY = 32
CONV_CARRY = SUBLANES
PAIRS_PER_KV = GROUP // 2
LOG2E = math.log2(math.e)
Q_SCALE = HEAD_DIM ** -0.5 * LOG2E
MASK_BIAS = 2.0 ** 100

_INV_FREQ = tuple(float(ROPE_THETA ** (-(2.0 * i) / ROT_DIMS)) for i in range(N_FREQ))

F32 = jnp.float32
BF16 = jnp.bfloat16


def _dot(a, b):
    return jnp.dot(a, b, preferred_element_type=F32)


def _rms_norm(x, g):
    ms = jnp.mean(x * x, axis=-1, keepdims=True)
    return x * lax.rsqrt(ms + NORM_EPS) * g


def _silu(z):
    return z * jax.nn.sigmoid(z)


def _rope_tables(pos_row, tm):
    sub = lax.broadcasted_iota(jnp.int32, (N_FREQ, tm), 0)
    inv_freq = jnp.full((N_FREQ, tm), _INV_FREQ[N_FREQ - 1], F32)
    for f in range(N_FREQ - 1):
        inv_freq = jnp.where(sub == f, _INV_FREQ[f], inv_freq)
    ang = pos_row.astype(F32) * inv_freq
    cos8 = jnp.cos(ang)
    sin8 = jnp.sin(ang)
    r = lax.broadcasted_iota(jnp.int32, (LANES, LANES), 0) & (HEAD_DIM - 1)
    c_blocks, s_blocks = [], []
    for blk in range(tm // LANES):
        c_rep = jnp.concatenate([cos8[:, blk * LANES:(blk + 1) * LANES]] * (LANES // N_FREQ), axis=0)
        s_rep = jnp.concatenate([sin8[:, blk * LANES:(blk + 1) * LANES]] * (LANES // N_FREQ), axis=0)
        c_pat = jnp.where(r < ROT_DIMS, c_rep, 1.0)
        s_pat = jnp.where(r < N_FREQ, -s_rep, jnp.where(r < ROT_DIMS, s_rep, 0.0))
        c_blocks.append(c_pat.T)
        s_blocks.append(s_pat.T)
    return jnp.concatenate(c_blocks, axis=0), jnp.concatenate(s_blocks, axis=0)


def _even_kernel(x_ref, g_ref, w_in_ref, w_pool_ref, ps_ref, conv_ref, w_out_ref, o_ref,
                 ubuf, s2buf, s4buf, cbuf, *, tm, tc):
    t = pl.program_id(1)
    n_slabs = D_MODEL // LANES
    slabs_per_group = POOL_GC // LANES
    n_s2, n_s4 = n_slabs - slabs_per_group, n_slabs - 2 * slabs_per_group
    ext = POOL_CARRY + tc

    @pl.when(t == 0)
    def _():
        ubuf[:, 0:POOL_CARRY, :] = jnp.zeros((n_slabs, POOL_CARRY, LANES), F32)
        cbuf[:, 0:CONV_CARRY, :] = jnp.zeros((n_slabs, CONV_CARRY, LANES), F32)

    def chunk(c):
        r0 = c * tc
        x = x_ref[0, r0:r0 + tc, :]
        y = _rms_norm(x, g_ref[...]).astype(BF16)
        u = _dot(y, w_in_ref[:, 0:D_MODEL])
        gate_c = _dot(y, w_in_ref[:, 2 * D_MODEL:3 * D_MODEL])
        h_c = _dot(y, w_in_ref[:, 3 * D_MODEL:4 * D_MODEL])
        gate_b = _dot(y, w_in_ref[:, D_MODEL:2 * D_MODEL])
        z_b = _dot(y, w_in_ref[:, 5 * D_MODEL:6 * D_MODEL])
        z_a = _dot(y, w_in_ref[:, 4 * D_MODEL:5 * D_MODEL])

        pos = t * tm + r0 + lax.broadcasted_iota(jnp.int32, (tc, LANES), 0)
        a_parts = []
        for g, w in enumerate(POOL_WINDOWS):
            inv = 1.0 / jnp.minimum(pos + 1, w).astype(F32)
            slabs = []
            for j in range(g * slabs_per_group, (g + 1) * slabs_per_group):
                u_j = u[:, j * LANES:(j + 1) * LANES]
                ubuf[j, r0 + POOL_CARRY:r0 + ext, :] = u_j
                s = ubuf[j, r0 + 8:r0 + ext, :] + ubuf[j, r0 + 7:r0 + ext - 1, :]
                if w >= 4:
                    jj = c * n_s2 + j - slabs_per_group
                    s2buf[jj, :, :] = s
                    s = s[8:, :] + s2buf[jj, 6:ext - 10, :]
                if w >= 8:
                    jj = c * n_s4 + j - 2 * slabs_per_group
                    s4buf[jj, :, :] = s
                    s = s[8:, :] + s4buf[jj, 4:ext - 20, :]
                if w >= 16:
                    s = s[8:, :] + s[:-8, :]
                slabs.append(s[s.shape[0] - tc:, :] * inv - u_j)
            pooled = jnp.concatenate(slabs, axis=1).astype(BF16)
            a_parts.append(_dot(pooled, w_pool_ref[g]))
        a = jnp.concatenate(a_parts, axis=1) * ps_ref[...]

        cu = gate_c * h_c
        cw = conv_ref[...]
        v_parts = []
        for j in range(n_slabs):
            lanes = slice(j * LANES, (j + 1) * LANES)
            cbuf[j, r0 + CONV_CARRY:r0 + CONV_CARRY + tc, :] = cu[:, lanes]
            v_parts.append(cw[2:3, lanes] * cu[:, lanes]
                           + cw[1:2, lanes] * cbuf[j, r0 + CONV_CARRY - 1:r0 + CONV_CARRY - 1 + tc, :]
                           + cw[0:1, lanes] * cbuf[j, r0 + CONV_CARRY - 2:r0 + CONV_CARRY - 2 + tc, :])
        bo = gate_b * jnp.concatenate(v_parts, axis=1)

        mix_b = (bo * _silu(z_b)).astype(BF16)
        mix_a = (a * _silu(z_a)).astype(BF16)
        o_ref[0, r0:r0 + tc, :] = x + (_dot(mix_b, w_out_ref[D_MODEL:, 0:D_MODEL])
                                       + _dot(mix_a, w_out_ref[0:D_MODEL, 0:D_MODEL]))

    for c in range(tm // tc):
        chunk(c)
    ubuf[:, 0:POOL_CARRY, :] = ubuf[:, tm:tm + POOL_CARRY, :]
    cbuf[:, 0:CONV_CARRY, :] = cbuf[:, tm:tm + CONV_CARRY, :]


def _const_spec(shape):
    nd = len(shape)
    return pl.BlockSpec(shape, lambda *_: (0,) * nd, pipeline_mode=pl.Buffered(1))


def _even_layer(x, g, w_in, w_pool, pool_scale, conv_w, w_out, *, tm, tc):
    b, s, d = x.shape
    tile = pl.BlockSpec((1, tm, d), lambda i, j: (i, j, 0))
    n_slabs, per_group = d // LANES, POOL_GC // LANES
    weight_bytes = 2 * (w_in.size + w_pool.size + w_out.size)
    tile_bytes = 4 * tm * d
    vmem_limit = weight_bytes + (4 + 3 + 10) * tile_bytes + (4 << 20)
    return pl.pallas_call(
        functools.partial(_even_kernel, tm=tm, tc=tc),
        grid=(b, s // tm),
        in_specs=[tile, _const_spec((1, d)), _const_spec(w_in.shape), _const_spec(w_pool.shape),
                  _const_spec((1, d)), _const_spec(conv_w.shape), _const_spec(w_out.shape)],
        out_specs=tile,
        out_shape=jax.ShapeDtypeStruct(x.shape, x.dtype),
        scratch_shapes=[pltpu.VMEM((n_slabs, POOL_CARRY + tm, LANES), F32),
                        pltpu.VMEM((tm // tc * (n_slabs - per_group), POOL_CARRY + tc - 8, LANES), F32),
                        pltpu.VMEM((tm // tc * (n_slabs - 2 * per_group), POOL_CARRY + tc - 16, LANES), F32),
                        pltpu.VMEM((n_slabs, CONV_CARRY + tm, LANES), F32)],
        compiler_params=pltpu.CompilerParams(
            dimension_semantics=("arbitrary", "arbitrary"), vmem_limit_bytes=vmem_limit),
        name="even_layer",
    )(x, g.reshape(1, d), w_in, w_pool, pool_scale.reshape(1, d), conv_w, w_out)


def _odd_kernel(sink_ref, h_ref, pos_ref, g_ref, w_in_ref, b_in_ref, w_out_ref, b_out_ref, gf_ref,
                o_ref, kx, vtx, *, tm):
    t = pl.program_id(1)
    nq = tm // WINDOW

    @pl.when(t == 0)
    def _():
        kx[:, 0:WINDOW, :] = jnp.zeros((2 * N_KV_HEADS, WINDOW, LANES), BF16)
        vtx[:, 0:WINDOW] = jnp.zeros((KV_WIDTH, WINDOW), BF16)

    h = h_ref[0]
    y = _rms_norm(h, g_ref[...]).astype(BF16)
    qkv_w = D_MODEL + 2 * KV_WIDTH
    kv = _dot(y, w_in_ref[:, D_MODEL:qkv_w]) + b_in_ref[:, D_MODEL:qkv_w]
    proj = _dot(y, w_in_ref[:, 0:D_MODEL]) + b_in_ref[:, 0:D_MODEL]
    k = kv[:, 0:KV_WIDTH]
    v = kv[:, KV_WIDTH:]

    c_tab, s_tab = _rope_tables(pos_ref[0], tm)
    lane = lax.broadcasted_iota(jnp.int32, (tm, LANES), 1)
    first_half = (lane & (HEAD_DIM - 1)) < N_FREQ

    def rope(xc, cc, sc):
        swapped = jnp.where(first_half, pltpu.roll(xc, LANES - N_FREQ, 1), pltpu.roll(xc, N_FREQ, 1))
        return xc * cc + swapped * sc

    cq = c_tab * Q_SCALE
    sq = s_tab * Q_SCALE
    qr = [rope(proj[:, j * LANES:(j + 1) * LANES], cq, sq).astype(BF16) for j in range(D_MODEL // LANES)]
    kr = rope(k, c_tab, s_tab)

    head0 = lane < HEAD_DIM
    lo0 = jnp.where(head0, kr, 0.0)
    hi1 = jnp.where(head0, 0.0, kr)
    kx[0, WINDOW:, :] = lo0.astype(BF16)
    kx[1, WINDOW:, :] = pltpu.roll(lo0, HEAD_DIM, 1).astype(BF16)
    kx[2, WINDOW:, :] = pltpu.roll(hi1, HEAD_DIM, 1).astype(BF16)
    kx[3, WINDOW:, :] = hi1.astype(BF16)
    for blk in range(nq):
        vtx[:, WINDOW + blk * WINDOW:WINDOW + (blk + 1) * WINDOW] = (
            v[blk * WINDOW:(blk + 1) * WINDOW, :].T.astype(BF16))

    slot = lax.broadcasted_iota(jnp.int32, (WINDOW, LANES), 0)
    qidx = lax.broadcasted_iota(jnp.int32, (WINDOW, LANES), 1)
    diff = slot - qidx
    eye = jnp.where(diff == 0, 1.0, 0.0).astype(BF16)
    masked = jnp.full((WINDOW, LANES), -MASK_BIAS, F32)
    bias_prev = jnp.where(diff > 0, 0.0, masked)
    bias_cur = jnp.where(diff <= 0, 0.0, masked)
    bias_mid = jnp.concatenate([bias_prev, bias_cur], axis=0).astype(BF16)
    bias_first = jnp.concatenate([jnp.where(t == 0, masked, bias_prev), bias_cur], axis=0).astype(BF16)
    r16 = lax.broadcasted_iota(jnp.int32, (2 * SUBLANES, 4 * WINDOW), 0)
    l16 = lax.broadcasted_iota(jnp.int32, (2 * SUBLANES, 4 * WINDOW), 1)
    ones_rows = jnp.where((r16 < SUBLANES) == (l16 < 2 * WINDOW), 1.0, 0.0).astype(BF16)
    zeros_v = jnp.zeros((HEAD_DIM, 2 * WINDOW), BF16)

    out_chunks = [[None] * nq for _ in range(D_MODEL // LANES)]
    units = [(qb, kv) for qb in range(nq) for kv in range(N_KV_HEADS)]

    def scores(qb, kv):
        a0 = qb * WINDOW
        bias = bias_first if qb == 0 else bias_mid
        w_ext = jnp.concatenate(
            [jnp.concatenate([kx[2 * kv + par, a0:a0 + 2 * WINDOW, :], bias], axis=1) for par in range(2)],
            axis=0)
        q_ext = jnp.concatenate(
            [jnp.concatenate([qr[kv * PAIRS_PER_KV + j][a0:a0 + WINDOW, :], eye], axis=1)
             for j in range(PAIRS_PER_KV)], axis=0)
        return lax.dot_general(w_ext, q_ext, (((1,), (1,)), ((), ())), preferred_element_type=F32)

    def attend(qb, kv, st):
        a0 = qb * WINDOW
        p_rows = []
        sink_terms = []
        for par in range(2):
            blocks = []
            terms = []
            for j in range(PAIRS_PER_KV):
                sink = sink_ref[kv * GROUP + 2 * j + par] * LOG2E
                blk_s = st[par * 2 * WINDOW:(par + 1) * 2 * WINDOW, j * WINDOW:(j + 1) * WINDOW]
                m = jnp.maximum(jnp.max(blk_s, axis=0, keepdims=True), sink)
                blocks.append(jnp.exp2(blk_s - m).astype(BF16))
                terms.append(jnp.exp2(sink - m))
            p_rows.append(jnp.concatenate(blocks, axis=1))
            sink_terms.append(jnp.concatenate(terms, axis=1))
        p_t = jnp.concatenate(p_rows, axis=0)
        v_win = vtx[kv * HEAD_DIM:(kv + 1) * HEAD_DIM, a0:a0 + 2 * WINDOW]
        vt_cat = jnp.concatenate([jnp.concatenate([v_win, zeros_v], axis=1),
                                  jnp.concatenate([zeros_v, v_win], axis=1),
                                  ones_rows], axis=0)
        o_t = _dot(vt_cat, p_t)
        den_e = o_t[2 * HEAD_DIM:2 * HEAD_DIM + SUBLANES, :] + sink_terms[0]
        den_o = o_t[2 * HEAD_DIM + SUBLANES:, :] + sink_terms[1]
        inv_e = jnp.concatenate([1.0 / den_e] * (HEAD_DIM // SUBLANES), axis=0)
        inv_o = jnp.concatenate([1.0 / den_o] * (HEAD_DIM // SUBLANES), axis=0)
        n_t = o_t[:2 * HEAD_DIM, :] * jnp.concatenate([inv_e, inv_o], axis=0)
        for j in range(PAIRS_PER_KV):
            out_chunks[kv * PAIRS_PER_KV + j][qb] = n_t[:, j * WINDOW:(j + 1) * WINDOW].T

    z_cols = 2 * LANES
    n_zc = D_MODEL // z_cols
    z_rows = tm * n_zc // len(units)
    z_parts = [[None] * n_zc for _ in range(tm // z_rows)]

    def gate_piece(i):
        r0, c0 = (i // n_zc) * z_rows, qkv_w + (i % n_zc) * z_cols
        z_parts[i // n_zc][i % n_zc] = (_dot(y[r0:r0 + z_rows, :], w_in_ref[:, c0:c0 + z_cols])
                                        + b_in_ref[:, c0:c0 + z_cols])

    gate_piece(0)
    st_next = scores(*units[0])
    for i, unit in enumerate(units):
        st = st_next
        if i + 1 < len(units):
            gate_piece(i + 1)
            st_next = scores(*units[i + 1])
        attend(*unit, st)

    kx[:, 0:WINDOW, :] = kx[:, tm:tm + WINDOW, :]
    vtx[:, 0:WINDOW] = vtx[:, tm:tm + WINDOW]

    z = jnp.concatenate([jnp.concatenate(row, axis=1) for row in z_parts], axis=0)
    attn = jnp.concatenate([jnp.concatenate(col, axis=0) for col in out_chunks], axis=1)
    gated = (attn * _silu(z)).astype(BF16)
    h2 = h + _dot(gated, w_out_ref[:, 0:D_MODEL]) + b_out_ref[...]
    o_ref[0] = _rms_norm(h2, gf_ref[...])


def _odd_layer(h, positions, g, w_in, b_in, sinks, w_out, b_out, g_final, *, tm):
    b, s, d = h.shape
    tile = pl.BlockSpec((1, tm, d), lambda i, j: (i, j, 0))
    pos_spec = pl.BlockSpec((1, 1, tm), lambda i, j: (i, 0, j))
    weight_bytes = 2 * (w_in.size + w_out.size)
    tile_bytes = 4 * tm * d
    vmem_limit = weight_bytes + (4 + 12) * tile_bytes + (8 << 20)
    return pl.pallas_call(
        functools.partial(_odd_kernel, tm=tm),
        grid=(b, s // tm),
        in_specs=[pl.BlockSpec(memory_space=pltpu.SMEM), tile, pos_spec, _const_spec((1, d)),
                  _const_spec(w_in.shape), _const_spec((1, ODD_IN)), _const_spec(w_out.shape),
                  _const_spec((1, d)), _const_spec((1, d))],
        out_specs=tile,
        out_shape=jax.ShapeDtypeStruct(h.shape, h.dtype),
        scratch_shapes=[pltpu.VMEM((2 * N_KV_HEADS, WINDOW + tm, LANES), BF16),
                        pltpu.VMEM((KV_WIDTH, WINDOW + tm), BF16)],
        compiler_params=pltpu.CompilerParams(
            dimension_semantics=("arbitrary", "arbitrary"), vmem_limit_bytes=vmem_limit),
        name="odd_layer",
    )(sinks, h, positions.reshape(b, 1, s), g.reshape(1, d), w_in, b_in.reshape(1, ODD_IN), w_out,
      b_out.reshape(1, d), g_final.reshape(1, d))


EVEN_TM = 512
EVEN_TC = 256
ODD_TM = 512


def _mxu_weight(w):
    w = w.astype(BF16)
    if (w.shape[-1] // LANES) % 2 == 0:
        w = jnp.pad(w, [(0, 0)] * (w.ndim - 1) + [(0, LANES)])
    return w


def kernel(x, positions, norm_g, w_in_even, w_pool, pool_scale, conv_w, w_out_even, w_in_odd, b_in_odd,
           attn_sinks, w_out_odd, b_out_odd, final_norm_g):
    assert norm_g.shape[0] == 2 and w_in_even.shape[0] == 1 and w_in_odd.shape[0] == 1
    h = _even_layer(x, norm_g[0], _mxu_weight(w_in_even[0]), w_pool[0].astype(BF16), pool_scale[0],
                    conv_w[0], _mxu_weight(w_out_even[0]), tm=EVEN_TM, tc=EVEN_TC)
    return _odd_layer(h, positions, norm_g[1], w_in_odd[0].astype(BF16), b_in_odd[0], attn_sinks[0],
                      _mxu_weight(w_out_odd[0]), b_out_odd[0], final_norm_g, tm=ODD_TM)
```

```python
import functools
import math

import jax
import jax.numpy as jnp
from jax import lax
from jax.experimental import pallas as pl
from jax.experimental.pallas import tpu as pltpu

D_MODEL = 1024
NORM_EPS = 1e-5
POOL_WINDOWS = (2, 4, 8, 16)
POOL_GC = D_MODEL // len(POOL_WINDOWS)
CONV_WIDTH = D_MODEL
EVEN_IN = 6 * D_MODEL
HEAD_DIM = 64
N_HEADS = 16
N_KV_HEADS = 2
GROUP = N_HEADS // N_KV_HEADS
KV_WIDTH = N_KV_HEADS * HEAD_DIM
ODD_IN = 2 * D_MODEL + 2 * KV_WIDTH
WINDOW = 128
ROPE_THETA = 500000.0
ROT_DIMS = HEAD_DIM // 4
N_FREQ = ROT_DIMS // 2

LANES = 128
SUBLANES = 8
POOL_CARRY = 32
CONV_CARRY = SUBLANES
PAIRS_PER_KV = GROUP // 2
LOG2E = math.log2(math.e)
Q_SCALE = HEAD_DIM ** -0.5 * LOG2E
MASK_BIAS = 2.0 ** 100
OUT_ROWS = 128

_INV_FREQ = tuple(float(ROPE_THETA ** (-(2.0 * i) / ROT_DIMS)) for i in range(N_FREQ))

F32 = jnp.float32
BF16 = jnp.bfloat16


def _dot(a, b):
    return jnp.dot(a, b, preferred_element_type=F32)


def _rms_norm(x, g):
    ms = jnp.mean(x * x, axis=-1, keepdims=True)
    return x * lax.rsqrt(ms + NORM_EPS) * g


def _silu(z):
    return z * jax.nn.sigmoid(z)


def _rope_tables(pos_row, tm):
    sub = lax.broadcasted_iota(jnp.int32, (N_FREQ, tm), 0)
    inv_freq = jnp.full((N_FREQ, tm), _INV_FREQ[N_FREQ - 1], F32)
    for f in range(N_FREQ - 1):
        inv_freq = jnp.where(sub == f, _INV_FREQ[f], inv_freq)
    ang = pos_row.astype(F32) * inv_freq
    cos8 = jnp.cos(ang)
    sin8 = jnp.sin(ang)
    r = lax.broadcasted_iota(jnp.int32, (LANES, LANES), 0) & (HEAD_DIM - 1)
    c_blocks, s_blocks = [], []
    for blk in range(tm // LANES):
        c_rep = jnp.concatenate([cos8[:, blk * LANES:(blk + 1) * LANES]] * (LANES // N_FREQ), axis=0)
        s_rep = jnp.concatenate([sin8[:, blk * LANES:(blk + 1) * LANES]] * (LANES // N_FREQ), axis=0)
        c_pat = jnp.where(r < ROT_DIMS, c_rep, 1.0)
        s_pat = jnp.where(r < N_FREQ, -s_rep, jnp.where(r < ROT_DIMS, s_rep, 0.0))
        c_blocks.append(c_pat.T)
        s_blocks.append(s_pat.T)
    return jnp.concatenate(c_blocks, axis=0), jnp.concatenate(s_blocks, axis=0)


def _even_kernel(x_ref, g_ref, w_in_ref, w_pool_ref, ps_ref, conv_ref, w_out_ref, o_ref,
                 ubuf, s2buf, s4buf, cbuf, *, tm, tc):
    t = pl.program_id(1)
    n_slabs = D_MODEL // LANES
    slabs_per_group = POOL_GC // LANES
    n_s2, n_s4 = n_slabs - slabs_per_group, n_slabs - 2 * slabs_per_group
    ext = POOL_CARRY + tc

    @pl.when(t == 0)
    def _():
        ubuf[:, 0:POOL_CARRY, :] = jnp.zeros((n_slabs, POOL_CARRY, LANES), F32)
        cbuf[:, 0:CONV_CARRY, :] = jnp.zeros((n_slabs, CONV_CARRY, LANES), F32)

    def chunk(c):
        r0 = c * tc
        x = x_ref[0, r0:r0 + tc, :]
        y = _rms_norm(x, g_ref[...]).astype(BF16)
        u = _dot(y, w_in_ref[:, 0:D_MODEL])
        gate_c = _dot(y, w_in_ref[:, 2 * D_MODEL:3 * D_MODEL])
        h_c = _dot(y, w_in_ref[:, 3 * D_MODEL:4 * D_MODEL])
        gate_b = _dot(y, w_in_ref[:, D_MODEL:2 * D_MODEL])
        z_b = _dot(y, w_in_ref[:, 5 * D_MODEL:6 * D_MODEL])
        z_a = _dot(y, w_in_ref[:, 4 * D_MODEL:5 * D_MODEL])

        pos = t * tm + r0 + lax.broadcasted_iota(jnp.int32, (tc, LANES), 0)
        a_parts = []
        for g, w in enumerate(POOL_WINDOWS):
            inv = 1.0 / jnp.minimum(pos + 1, w).astype(F32)
            slabs = []
            for j in range(g * slabs_per_group, (g + 1) * slabs_per_group):
                u_j = u[:, j * LANES:(j + 1) * LANES]
                ubuf[j, r0 + POOL_CARRY:r0 + ext, :] = u_j
                s = ubuf[j, r0 + 8:r0 + ext, :] + ubuf[j, r0 + 7:r0 + ext - 1, :]
                if w >= 4:
                    jj = c * n_s2 + j - slabs_per_group
                    s2buf[jj, :, :] = s
                    s = s[8:, :] + s2buf[jj, 6:ext - 10, :]
                if w >= 8:
                    jj = c * n_s4 + j - 2 * slabs_per_group
                    s4buf[jj, :, :] = s
                    s = s[8:, :] + s4buf[jj, 4:ext - 20, :]
                if w >= 16:
                    s = s[8:, :] + s[:-8, :]
                slabs.append(s[s.shape[0] - tc:, :] * inv - u_j)
            pooled = jnp.concatenate(slabs, axis=1).astype(BF16)
            a_parts.append(_dot(pooled, w_pool_ref[g]))
        a = jnp.concatenate(a_parts, axis=1) * ps_ref[...]

        cu = gate_c * h_c
        cw = conv_ref[...]
        v_parts = []
        for j in range(n_slabs):
            lanes = slice(j * LANES, (j + 1) * LANES)
            cbuf[j, r0 + CONV_CARRY:r0 + CONV_CARRY + tc, :] = cu[:, lanes]
            v_parts.append(cw[2:3, lanes] * cu[:, lanes]
                           + cw[1:2, lanes] * cbuf[j, r0 + CONV_CARRY - 1:r0 + CONV_CARRY - 1 + tc, :]
                           + cw[0:1, lanes] * cbuf[j, r0 + CONV_CARRY - 2:r0 + CONV_CARRY - 2 + tc, :])
        bo = gate_b * jnp.concatenate(v_parts, axis=1)

        mix_b = (bo * _silu(z_b)).astype(BF16)
        mix_a = (a * _silu(z_a)).astype(BF16)
        o_ref[0, r0:r0 + tc, :] = x + (_dot(mix_b, w_out_ref[D_MODEL:, 0:D_MODEL])
                                       + _dot(mix_a, w_out_ref[0:D_MODEL, 0:D_MODEL]))

    for c in range(tm // tc):
        chunk(c)
    ubuf[:, 0:POOL_CARRY, :] = ubuf[:, tm:tm + POOL_CARRY, :]
    cbuf[:, 0:CONV_CARRY, :] = cbuf[:, tm:tm + CONV_CARRY, :]


def _const_spec(shape):
    nd = len(shape)
    return pl.BlockSpec(shape, lambda *_: (0,) * nd, pipeline_mode=pl.Buffered(1))


def _even_layer(x, g, w_in, w_pool, pool_scale, conv_w, w_out, *, tm, tc):
    b, s, d = x.shape
    tile = pl.BlockSpec((1, tm, d), lambda i, j: (i, j, 0))
    n_slabs, per_group = d // LANES, POOL_GC // LANES
    weight_bytes = 2 * (w_in.size + w_pool.size + w_out.size)
    tile_bytes = 4 * tm * d
    vmem_limit = weight_bytes + (4 + 3 + 10) * tile_bytes + (4 << 20)
    return pl.pallas_call(
        functools.partial(_even_kernel, tm=tm, tc=tc),
        grid=(b, s // tm),
        in_specs=[tile, _const_spec((1, d)), _const_spec(w_in.shape), _const_spec(w_pool.shape),
                  _const_spec((1, d)), _const_spec(conv_w.shape), _const_spec(w_out.shape)],
        out_specs=tile,
        out_shape=jax.ShapeDtypeStruct(x.shape, x.dtype),
        scratch_shapes=[pltpu.VMEM((n_slabs, POOL_CARRY + tm, LANES), F32),
                        pltpu.VMEM((tm // tc * (n_slabs - per_group), POOL_CARRY + tc - 8, LANES), F32),
                        pltpu.VMEM((tm // tc * (n_slabs - 2 * per_group), POOL_CARRY + tc - 16, LANES), F32),
                        pltpu.VMEM((n_slabs, CONV_CARRY + tm, LANES), F32)],
        compiler_params=pltpu.CompilerParams(
            dimension_semantics=("arbitrary", "arbitrary"), vmem_limit_bytes=vmem_limit),
        name="even_layer",
    )(x, g.reshape(1, d), w_in, w_pool, pool_scale.reshape(1, d), conv_w, w_out)


def _odd_kernel(sink_ref, h_ref, pos_ref, g_ref, w_in_ref, b_in_ref, w_out_ref, b_out_ref, gf_ref,
                o_ref, kx, vtx, *, tm, tc):
    t = pl.program_id(1)
    nq = tc // WINDOW
    qkv_w = D_MODEL + 2 * KV_WIDTH

    @pl.when(t == 0)
    def _():
        kx[:, 0:WINDOW, :] = jnp.zeros((2 * N_KV_HEADS, WINDOW, LANES), BF16)
        vtx[:, 0:WINDOW] = jnp.zeros((KV_WIDTH, WINDOW), BF16)

    lane = lax.broadcasted_iota(jnp.int32, (tc, LANES), 1)
    first_half = (lane & (HEAD_DIM - 1)) < N_FREQ
    head0 = lane < HEAD_DIM

    def rope(xc, cc, sc):
        swapped = jnp.where(first_half, pltpu.roll(xc, LANES - N_FREQ, 1), pltpu.roll(xc, N_FREQ, 1))
        return xc * cc + swapped * sc

    slot = lax.broadcasted_iota(jnp.int32, (WINDOW, LANES), 0)
    qidx = lax.broadcasted_iota(jnp.int32, (WINDOW, LANES), 1)
    diff = slot - qidx
    eye = jnp.where(diff == 0, 1.0, 0.0).astype(BF16)
    masked = jnp.full((WINDOW, LANES), -MASK_BIAS, F32)
    bias_prev = jnp.where(diff > 0, 0.0, masked)
    bias_cur = jnp.where(diff <= 0, 0.0, masked)
    bias_mid = jnp.concatenate([bias_prev, bias_cur], axis=0).astype(BF16)
    bias_first = jnp.concatenate([jnp.where(t == 0, masked, bias_prev), bias_cur], axis=0).astype(BF16)
    ones_rows = jnp.ones((2 * SUBLANES, 2 * WINDOW), BF16)

    def chunk(c):
        r0 = c * tc
        h = h_ref[0, r0:r0 + tc, :]
        y = _rms_norm(h, g_ref[...]).astype(BF16)
        kv_p = _dot(y, w_in_ref[:, D_MODEL:qkv_w]) + b_in_ref[:, D_MODEL:qkv_w]
        proj = _dot(y, w_in_ref[:, 0:D_MODEL]) + b_in_ref[:, 0:D_MODEL]
        k = kv_p[:, 0:KV_WIDTH]
        v = kv_p[:, KV_WIDTH:]

        c_tab, s_tab = _rope_tables(pos_ref[0, :, r0:r0 + tc], tc)
        cq = c_tab * Q_SCALE
        sq = s_tab * Q_SCALE
        qr = [rope(proj[:, j * LANES:(j + 1) * LANES], cq, sq).astype(BF16) for j in range(D_MODEL // LANES)]
        kr = rope(k, c_tab, s_tab)

        k0, k1 = WINDOW + r0, WINDOW + r0 + tc
        lo0 = jnp.where(head0, kr, 0.0)
        hi1 = jnp.where(head0, 0.0, kr)
        kx[0, k0:k1, :] = lo0.astype(BF16)
        kx[1, k0:k1, :] = pltpu.roll(lo0, HEAD_DIM, 1).astype(BF16)
        kx[2, k0:k1, :] = pltpu.roll(hi1, HEAD_DIM, 1).astype(BF16)
        kx[3, k0:k1, :] = hi1.astype(BF16)
        for blk in range(nq):
            vtx[:, k0 + blk * WINDOW:k0 + (blk + 1) * WINDOW] = v[blk * WINDOW:(blk + 1) * WINDOW, :].T.astype(BF16)

        out_chunks = [[None] * nq for _ in range(D_MODEL // LANES)]
        units = [(qb, kv) for qb in range(nq) for kv in range(N_KV_HEADS)]

        def scores(qb, kv):
            a0 = r0 + qb * WINDOW
            bias = bias_first if a0 == 0 else bias_mid
            w_ext = jnp.concatenate(
                [jnp.concatenate([kx[2 * kv + par, a0:a0 + 2 * WINDOW, :], bias], axis=1) for par in range(2)],
                axis=0)
            q_ext = jnp.concatenate(
                [jnp.concatenate([qr[kv * PAIRS_PER_KV + j][qb * WINDOW:(qb + 1) * WINDOW, :], eye], axis=1)
                 for j in range(PAIRS_PER_KV)], axis=0)
            return lax.dot_general(w_ext, q_ext, (((1,), (1,)), ((), ())), preferred_element_type=F32)

        def attend(qb, kv, st):
            a0 = r0 + qb * WINDOW
            blocks = []
            terms = []
            for j in range(PAIRS_PER_KV):
                for par in range(2):
                    sink = sink_ref[kv * GROUP + 2 * j + par] * LOG2E
                    blk_s = st[par * 2 * WINDOW:(par + 1) * 2 * WINDOW, j * WINDOW:(j + 1) * WINDOW]
                    m = jnp.maximum(jnp.max(blk_s, axis=0, keepdims=True), sink)
                    blocks.append(jnp.exp2(blk_s - m).astype(BF16))
                    terms.append(jnp.exp2(sink - m))
            p_t = jnp.concatenate(blocks, axis=1)
            v_win = vtx[kv * HEAD_DIM:(kv + 1) * HEAD_DIM, a0:a0 + 2 * WINDOW]
            o_t = _dot(jnp.concatenate([v_win, ones_rows], axis=0), p_t)
            den = o_t[HEAD_DIM:HEAD_DIM + SUBLANES, :] + jnp.concatenate(terms, axis=1)
            n_t = o_t[:HEAD_DIM, :] * jnp.concatenate([1.0 / den] * (HEAD_DIM // SUBLANES), axis=0)
            for j in range(PAIRS_PER_KV):
                pair = jnp.concatenate([n_t[:, 2 * j * WINDOW:(2 * j + 1) * WINDOW],
                                        n_t[:, (2 * j + 1) * WINDOW:(2 * j + 2) * WINDOW]], axis=0)
                out_chunks[kv * PAIRS_PER_KV + j][qb] = pair.T

        z_cols = max(2 * LANES, D_MODEL // len(units))
        n_zc = D_MODEL // z_cols
        z_rows = tc * n_zc // len(units)
        z_parts = [[None] * n_zc for _ in range(tc // z_rows)]

        def gate_piece(i):
            z0, c0 = (i // n_zc) * z_rows, qkv_w + (i % n_zc) * z_cols
            z_parts[i // n_zc][i % n_zc] = (_dot(y[z0:z0 + z_rows, :], w_in_ref[:, c0:c0 + z_cols])
                                            + b_in_ref[:, c0:c0 + z_cols])

        gate_piece(0)
        st_next = scores(*units[0])
        for i, unit in enumerate(units):
            st = st_next
            if i + 1 < len(units):
                gate_piece(i + 1)
                st_next = scores(*units[i + 1])
            attend(*unit, st)

        z = jnp.concatenate([jnp.concatenate(row, axis=1) for row in z_parts], axis=0)
        attn = jnp.concatenate([jnp.concatenate(col, axis=0) for col in out_chunks], axis=1)
        gated = (attn * _silu(z)).astype(BF16)
        for p0 in range(0, tc, OUT_ROWS):
            h2 = h[p0:p0 + OUT_ROWS, :] + _dot(gated[p0:p0 + OUT_ROWS, :], w_out_ref[:, 0:D_MODEL]) + b_out_ref[...]
            o_ref[0, r0 + p0:r0 + p0 + OUT_ROWS, :] = _rms_norm(h2, gf_ref[...])

    for c in range(tm // tc):
        chunk(c)
    kx[:, 0:WINDOW, :] = kx[:, tm:tm + WINDOW, :]
    vtx[:, 0:WINDOW] = vtx[:, tm:tm + WINDOW]


def _odd_layer(h, positions, g, w_in, b_in, sinks, w_out, b_out, g_final, *, tm, tc):
    b, s, d = h.shape
    tile = pl.BlockSpec((1, tm, d), lambda i, j: (i, j, 0))
    pos_spec = pl.BlockSpec((1, 1, tm), lambda i, j: (i, 0, j))
    weight_bytes = 2 * (w_in.size + w_out.size)
    tile_bytes = 4 * tm * d
    vmem_limit = weight_bytes + (4 + 12) * tile_bytes + (8 << 20)
    return pl.pallas_call(
        functools.partial(_odd_kernel, tm=tm, tc=tc),
        grid=(b, s // tm),
        in_specs=[pl.BlockSpec(memory_space=pltpu.SMEM), tile, pos_spec, _const_spec((1, d)),
                  _const_spec(w_in.shape), _const_spec((1, ODD_IN)), _const_spec(w_out.shape),
                  _const_spec((1, d)), _const_spec((1, d))],
        out_specs=tile,
        out_shape=jax.ShapeDtypeStruct(h.shape, h.dtype),
        scratch_shapes=[pltpu.VMEM((2 * N_KV_HEADS, WINDOW + tm, LANES), BF16),
                        pltpu.VMEM((KV_WIDTH, WINDOW + tm), BF16)],
        compiler_params=pltpu.CompilerParams(
            dimension_semantics=("arbitrary", "arbitrary"), vmem_limit_bytes=vmem_limit),
        name="odd_layer",
    )(sinks, h, positions.reshape(b, 1, s), g.reshape(1, d), w_in, b_in.reshape(1, ODD_IN), w_out,
      b_out.reshape(1, d), g_final.reshape(1, d))


EVEN_TM = 512
EVEN_TC = 256
ODD_TM = 512
ODD_TC = 256


def _mxu_weight(w):
    if (w.shape[-1] // LANES) % 2 == 0:
        w = jnp.pad(w, [(0, 0)] * (w.ndim - 1) + [(0, LANES)])
    return w.astype(BF16)


def kernel(x, positions, norm_g, w_in_even, w_pool, pool_scale, conv_w, w_out_even, w_in_odd, b_in_odd,
           attn_sinks, w_out_odd, b_out_odd, final_norm_g):
    assert norm_g.shape[0] == 2 and w_in_even.shape[0] == 1 and w_in_odd.shape[0] == 1
    h = _even_layer(x, norm_g[0], _mxu_weight(w_in_even[0]), w_pool[0].astype(BF16), pool_scale[0],
                    conv_w[0], _mxu_weight(w_out_even[0]), tm=EVEN_TM, tc=EVEN_TC)
    return _odd_layer(h, positions, norm_g[1], w_in_odd[0].astype(BF16), b_in_odd[0], attn_sinks[0],
                      _mxu_weight(w_out_odd[0]), b_out_odd[0], final_norm_g, tm=ODD_TM, tc=ODD_TC)
```

```python
import functools
import math

import jax
import jax.numpy as jnp
from jax import lax
from jax.experimental import pallas as pl
from jax.experimental.pallas import tpu as pltpu

D_MODEL = 1024
NORM_EPS = 1e-5
POOL_WINDOWS = (2, 4, 8, 16)
POOL_GC = D_MODEL // len(POOL_WINDOWS)
CONV_WIDTH = D_MODEL
EVEN_IN = 6 * D_MODEL
HEAD_DIM = 64
N_HEADS = 16
N_KV_HEADS = 2
GROUP = N_HEADS // N_KV_HEADS
KV_WIDTH = N_KV_HEADS * HEAD_DIM
ODD_IN = 2 * D_MODEL + 2 * KV_WIDTH
WINDOW = 128
ROPE_THETA = 500000.0
ROT_DIMS = HEAD_DIM // 4
N_FREQ = ROT_DIMS // 2

VMEM_BYTES = 64 << 20
LANES = 128
SUBLANES = 8
POOL_CARRY = 32
CONV_CARRY = SUBLANES
PAIRS_PER_KV = GROUP // 2
LOG2E = math.log2(math.e)
Q_SCALE = HEAD_DIM ** -0.5 * LOG2E
MASK_BIAS = 2.0 ** 100
STAGE_ROWS = 128
OUT_ROWS = 256

_INV_FREQ = tuple(float(ROPE_THETA ** (-(2.0 * i) / ROT_DIMS)) for i in range(N_FREQ))

F32 = jnp.float32
BF16 = jnp.bfloat16


def _dot(a, b):
    return jnp.dot(a, b, preferred_element_type=F32)


def _rms_norm(x, g):
    ms = jnp.mean(x * x, axis=-1, keepdims=True)
    return x * lax.rsqrt(ms + NORM_EPS) * g


def _silu(z):
    return z * jax.nn.sigmoid(z)


def _rope_tables(pos_row, tm):
    sub = lax.broadcasted_iota(jnp.int32, (N_FREQ, tm), 0)
    inv_freq = jnp.full((N_FREQ, tm), _INV_FREQ[N_FREQ - 1], F32)
    for f in range(N_FREQ - 1):
        inv_freq = jnp.where(sub == f, _INV_FREQ[f], inv_freq)
    ang = pos_row.astype(F32) * inv_freq
    cos8 = jnp.cos(ang)
    sin8 = jnp.sin(ang)
    r = lax.broadcasted_iota(jnp.int32, (LANES, LANES), 0) & (HEAD_DIM - 1)
    c_blocks, s_blocks = [], []
    for blk in range(tm // LANES):
        c_rep = jnp.concatenate([cos8[:, blk * LANES:(blk + 1) * LANES]] * (LANES // N_FREQ), axis=0)
        s_rep = jnp.concatenate([sin8[:, blk * LANES:(blk + 1) * LANES]] * (LANES // N_FREQ), axis=0)
        c_pat = jnp.where(r < ROT_DIMS, c_rep, 1.0)
        s_pat = jnp.where(r < N_FREQ, -s_rep, jnp.where(r < ROT_DIMS, s_rep, 0.0))
        c_blocks.append(c_pat.T)
        s_blocks.append(s_pat.T)
    return jnp.concatenate(c_blocks, axis=0), jnp.concatenate(s_blocks, axis=0)


def _load_weight(src_hbm, dst, stage, sem):
    k_rows, width = src_hbm.shape
    n_blocks = k_rows // STAGE_ROWS

    def copy(k):
        return pltpu.make_async_copy(src_hbm.at[pl.ds(k * STAGE_ROWS, STAGE_ROWS), :], stage.at[k % 2], sem.at[k % 2])

    copy(0).start()
    for k in range(n_blocks):
        if k + 1 < n_blocks:
            copy(k + 1).start()
        copy(k).wait()
        dst[k * STAGE_ROWS:(k + 1) * STAGE_ROWS, 0:width] = stage[k % 2].astype(BF16)


def _first_step():
    return (pl.program_id(0) == 0) & (pl.program_id(1) == 0)


def _even_kernel(x_ref, g_ref, w_in_hbm, w_pool_hbm, ps_ref, conv_ref, w_out_hbm, o_ref,
                 ubuf, s2buf, s4buf, cbuf, w_in_ref, w_pool_ref, w_out_ref, stage_in, stage_pool, stage_out,
                 sem_in, sem_pool, sem_out, *, tm, tc):
    t = pl.program_id(1)
    n_slabs = D_MODEL // LANES
    slabs_per_group = POOL_GC // LANES
    n_s2, n_s4 = n_slabs - slabs_per_group, n_slabs - 2 * slabs_per_group
    ext = POOL_CARRY + tc

    @pl.when(_first_step())
    def _():
        _load_weight(w_in_hbm, w_in_ref, stage_in, sem_in)
        _load_weight(w_out_hbm, w_out_ref, stage_out, sem_out)
        _load_weight(w_pool_hbm, w_pool_ref, stage_pool, sem_pool)

    @pl.when(t == 0)
    def _():
        ubuf[:, 0:POOL_CARRY, :] = jnp.zeros((n_slabs, POOL_CARRY, LANES), F32)
        cbuf[:, 0:CONV_CARRY, :] = jnp.zeros((n_slabs, CONV_CARRY, LANES), F32)

    def chunk(c):
        r0 = c * tc
        x = x_ref[0, r0:r0 + tc, :]
        y = _rms_norm(x, g_ref[...]).astype(BF16)
        u = _dot(y, w_in_ref[:, 0:D_MODEL])
        gate_c = _dot(y, w_in_ref[:, 2 * D_MODEL:3 * D_MODEL])
        h_c = _dot(y, w_in_ref[:, 3 * D_MODEL:4 * D_MODEL])
        gate_b = _dot(y, w_in_ref[:, D_MODEL:2 * D_MODEL])
        z_b = _dot(y, w_in_ref[:, 5 * D_MODEL:6 * D_MODEL])
        z_a = _dot(y, w_in_ref[:, 4 * D_MODEL:5 * D_MODEL])

        pos = t * tm + r0 + lax.broadcasted_iota(jnp.int32, (tc, LANES), 0)
        a_parts = []
        for g, w in enumerate(POOL_WINDOWS):
            inv = 1.0 / jnp.minimum(pos + 1, w).astype(F32)
            slabs = []
            for j in range(g * slabs_per_group, (g + 1) * slabs_per_group):
                u_j = u[:, j * LANES:(j + 1) * LANES]
                ubuf[j, r0 + POOL_CARRY:r0 + ext, :] = u_j
                s = ubuf[j, r0 + 8:r0 + ext, :] + ubuf[j, r0 + 7:r0 + ext - 1, :]
                if w >= 4:
                    jj = c * n_s2 + j - slabs_per_group
                    s2buf[jj, :, :] = s
                    s = s[8:, :] + s2buf[jj, 6:ext - 10, :]
                if w >= 8:
                    jj = c * n_s4 + j - 2 * slabs_per_group
                    s4buf[jj, :, :] = s
                    s = s[8:, :] + s4buf[jj, 4:ext - 20, :]
                if w >= 16:
                    s = s[8:, :] + s[:-8, :]
                slabs.append(s[s.shape[0] - tc:, :] * inv - u_j)
            pooled = jnp.concatenate(slabs, axis=1).astype(BF16)
            a_parts.append(_dot(pooled, w_pool_ref[g * POOL_GC:(g + 1) * POOL_GC, :]))
        a = jnp.concatenate(a_parts, axis=1) * ps_ref[...]

        cu = gate_c * h_c
        cw = conv_ref[...]
        v_parts = []
        for j in range(n_slabs):
            lanes = slice(j * LANES, (j + 1) * LANES)
            cbuf[j, r0 + CONV_CARRY:r0 + CONV_CARRY + tc, :] = cu[:, lanes]
            v_parts.append(cw[2:3, lanes] * cu[:, lanes]
                           + cw[1:2, lanes] * cbuf[j, r0 + CONV_CARRY - 1:r0 + CONV_CARRY - 1 + tc, :]
                           + cw[0:1, lanes] * cbuf[j, r0 + CONV_CARRY - 2:r0 + CONV_CARRY - 2 + tc, :])
        bo = gate_b * jnp.concatenate(v_parts, axis=1)

        mix_b = (bo * _silu(z_b)).astype(BF16)
        mix_a = (a * _silu(z_a)).astype(BF16)
        o_ref[0, r0:r0 + tc, :] = x + (_dot(mix_b, w_out_ref[D_MODEL:, 0:D_MODEL])
                                       + _dot(mix_a, w_out_ref[0:D_MODEL, 0:D_MODEL]))

    for c in range(tm // tc):
        chunk(c)
    ubuf[:, 0:POOL_CARRY, :] = ubuf[:, tm:tm + POOL_CARRY, :]
    cbuf[:, 0:CONV_CARRY, :] = cbuf[:, tm:tm + CONV_CARRY, :]


def _const_spec(shape):
    nd = len(shape)
    return pl.BlockSpec(shape, lambda *_: (0,) * nd, pipeline_mode=pl.Buffered(1))


def _padded(shape):
    k_rows, width = shape
    return (k_rows, width + LANES) if (width // LANES) % 2 == 0 else (k_rows, width)


def _even_layer(x, g, w_in, w_pool, pool_scale, conv_w, w_out, *, tm, tc):
    b, s, d = x.shape
    tile = pl.BlockSpec((1, tm, d), lambda i, j: (i, j, 0))
    n_slabs, per_group = d // LANES, POOL_GC // LANES
    w_pool = w_pool.reshape(-1, w_pool.shape[-1])
    weight_bytes = 2 * (w_in.size + w_pool.size + w_out.size)
    stage_bytes = 2 * 4 * STAGE_ROWS * (w_in.shape[1] + w_pool.shape[1] + w_out.shape[1])
    tile_bytes, chunk_bytes = 4 * tm * d, 4 * tc * d
    assert weight_bytes + stage_bytes + (4 + 2) * tile_bytes + (3 + 10) * chunk_bytes <= VMEM_BYTES
    vmem_limit = VMEM_BYTES
    hbm = pl.BlockSpec(memory_space=pl.ANY)
    return pl.pallas_call(
        functools.partial(_even_kernel, tm=tm, tc=tc),
        grid=(b, s // tm),
        in_specs=[tile, _const_spec((1, d)), hbm, hbm, _const_spec((1, d)), _const_spec(conv_w.shape), hbm],
        out_specs=tile,
        out_shape=jax.ShapeDtypeStruct(x.shape, x.dtype),
        scratch_shapes=[pltpu.VMEM((n_slabs, POOL_CARRY + tm, LANES), F32),
                        pltpu.VMEM((tm // tc * (n_slabs - per_group), POOL_CARRY + tc - 8, LANES), F32),
                        pltpu.VMEM((tm // tc * (n_slabs - 2 * per_group), POOL_CARRY + tc - 16, LANES), F32),
                        pltpu.VMEM((n_slabs, CONV_CARRY + tm, LANES), F32),
                        pltpu.VMEM(_padded(w_in.shape), BF16),
                        pltpu.VMEM(w_pool.shape, BF16),
                        pltpu.VMEM(_padded(w_out.shape), BF16),
                        pltpu.VMEM((2, STAGE_ROWS, w_in.shape[1]), F32),
                        pltpu.VMEM((2, STAGE_ROWS, w_pool.shape[1]), F32),
                        pltpu.VMEM((2, STAGE_ROWS, w_out.shape[1]), F32),
                        pltpu.SemaphoreType.DMA((2,)), pltpu.SemaphoreType.DMA((2,)), pltpu.SemaphoreType.DMA((2,))],
        compiler_params=pltpu.CompilerParams(
            dimension_semantics=("arbitrary", "arbitrary"), vmem_limit_bytes=vmem_limit),
        name="even_layer",
    )(x, g.reshape(1, d), w_in, w_pool, pool_scale.reshape(1, d), conv_w, w_out)


def _odd_kernel(sink_ref, h_ref, pos_ref, g_ref, w_in_hbm, b_in_ref, w_out_hbm, b_out_ref, gf_ref,
                o_ref, kx, vtx, w_in_ref, w_out_ref, stage_in, stage_out, sem_in, sem_out, *, tm, tc):
    t = pl.program_id(1)
    nq = tc // WINDOW
    qkv_w = D_MODEL + 2 * KV_WIDTH

    @pl.when(_first_step())
    def _():
        _load_weight(w_in_hbm, w_in_ref, stage_in, sem_in)
        _load_weight(w_out_hbm, w_out_ref, stage_out, sem_out)

    @pl.when(t == 0)
    def _():
        kx[:, 0:WINDOW, :] = jnp.zeros((2 * N_KV_HEADS, WINDOW, LANES), BF16)
        vtx[:, 0:WINDOW] = jnp.zeros((KV_WIDTH, WINDOW), BF16)

    lane = lax.broadcasted_iota(jnp.int32, (tc, LANES), 1)
    first_half = (lane & (HEAD_DIM - 1)) < N_FREQ
    head0 = lane < HEAD_DIM

    def rope(xc, cc, sc):
        swapped = jnp.where(first_half, pltpu.roll(xc, LANES - N_FREQ, 1), pltpu.roll(xc, N_FREQ, 1))
        return xc * cc + swapped * sc

    slot = lax.broadcasted_iota(jnp.int32, (WINDOW, LANES), 0)
    qidx = lax.broadcasted_iota(jnp.int32, (WINDOW, LANES), 1)
    diff = slot - qidx
    eye = jnp.where(diff == 0, 1.0, 0.0).astype(BF16)
    masked = jnp.full((WINDOW, LANES), -MASK_BIAS, F32)
    bias_prev = jnp.where(diff > 0, 0.0, masked)
    bias_cur = jnp.where(diff <= 0, 0.0, masked)
    bias_mid = jnp.concatenate([bias_prev, bias_cur], axis=0).astype(BF16)
    bias_first = jnp.concatenate([jnp.where(t == 0, masked, bias_prev), bias_cur], axis=0).astype(BF16)
    ones_rows = jnp.ones((2 * SUBLANES, 2 * WINDOW), BF16)

    def chunk(c):
        r0 = c * tc
        h = h_ref[0, r0:r0 + tc, :]
        y = _rms_norm(h, g_ref[...]).astype(BF16)
        kv_p = _dot(y, w_in_ref[:, D_MODEL:qkv_w]) + b_in_ref[:, D_MODEL:qkv_w]
        proj = _dot(y, w_in_ref[:, 0:D_MODEL]) + b_in_ref[:, 0:D_MODEL]
        k = kv_p[:, 0:KV_WIDTH]
        v = kv_p[:, KV_WIDTH:]

        c_tab, s_tab = _rope_tables(pos_ref[0, :, r0:r0 + tc], tc)
        cq = c_tab * Q_SCALE
        sq = s_tab * Q_SCALE
        qr = [rope(proj[:, j * LANES:(j + 1) * LANES], cq, sq).astype(BF16) for j in range(D_MODEL // LANES)]
        kr = rope(k, c_tab, s_tab)

        k0, k1 = WINDOW + r0, WINDOW + r0 + tc
        lo0 = jnp.where(head0, kr, 0.0)
        hi1 = jnp.where(head0, 0.0, kr)
        kx[0, k0:k1, :] = lo0.astype(BF16)
        kx[1, k0:k1, :] = pltpu.roll(lo0, HEAD_DIM, 1).astype(BF16)
        kx[2, k0:k1, :] = pltpu.roll(hi1, HEAD_DIM, 1).astype(BF16)
        kx[3, k0:k1, :] = hi1.astype(BF16)
        for blk in range(nq):
            vtx[:, k0 + blk * WINDOW:k0 + (blk + 1) * WINDOW] = v[blk * WINDOW:(blk + 1) * WINDOW, :].T.astype(BF16)

        out_chunks = [[None] * nq for _ in range(D_MODEL // LANES)]
        units = [(qb, kv) for qb in range(nq) for kv in range(N_KV_HEADS)]

        def scores(qb, kv):
            a0 = r0 + qb * WINDOW
            bias = bias_first if a0 == 0 else bias_mid
            w_ext = jnp.concatenate(
                [jnp.concatenate([kx[2 * kv + par, a0:a0 + 2 * WINDOW, :], bias], axis=1) for par in range(2)],
                axis=0)
            q_ext = jnp.concatenate(
                [jnp.concatenate([qr[kv * PAIRS_PER_KV + j][qb * WINDOW:(qb + 1) * WINDOW, :], eye], axis=1)
                 for j in range(PAIRS_PER_KV)], axis=0)
            return lax.dot_general(w_ext, q_ext, (((1,), (1,)), ((), ())), preferred_element_type=F32)

        def attend(qb, kv, st):
            a0 = r0 + qb * WINDOW
            blocks = []
            terms = []
            for j in range(PAIRS_PER_KV):
                for par in range(2):
                    sink = sink_ref[kv * GROUP + 2 * j + par] * LOG2E
                    blk_s = st[par * 2 * WINDOW:(par + 1) * 2 * WINDOW, j * WINDOW:(j + 1) * WINDOW]
                    m = jnp.maximum(jnp.max(blk_s, axis=0, keepdims=True), sink)
                    blocks.append(jnp.exp2(blk_s - m).astype(BF16))
                    terms.append(jnp.exp2(sink - m))
            p_t = jnp.concatenate(blocks, axis=1)
            v_win = vtx[kv * HEAD_DIM:(kv + 1) * HEAD_DIM, a0:a0 + 2 * WINDOW]
            o_t = _dot(jnp.concatenate([v_win, ones_rows], axis=0), p_t)
            den = o_t[HEAD_DIM:HEAD_DIM + SUBLANES, :] + jnp.concatenate(terms, axis=1)
            n_t = o_t[:HEAD_DIM, :] * jnp.concatenate([1.0 / den] * (HEAD_DIM // SUBLANES), axis=0)
            for j in range(PAIRS_PER_KV):
                pair = jnp.concatenate([n_t[:, 2 * j * WINDOW:(2 * j + 1) * WINDOW],
                                        n_t[:, (2 * j + 1) * WINDOW:(2 * j + 2) * WINDOW]], axis=0)
                out_chunks[kv * PAIRS_PER_KV + j][qb] = pair.T

        z_cols = max(2 * LANES, D_MODEL // len(units))
        n_zc = D_MODEL // z_cols
        z_rows = tc * n_zc // len(units)
        z_parts = [[None] * n_zc for _ in range(tc // z_rows)]

        def gate_piece(i):
            z0, c0 = (i // n_zc) * z_rows, qkv_w + (i % n_zc) * z_cols
            z_parts[i // n_zc][i % n_zc] = (_dot(y[z0:z0 + z_rows, :], w_in_ref[:, c0:c0 + z_cols])
                                            + b_in_ref[:, c0:c0 + z_cols])

        gate_piece(0)
        st_next = scores(*units[0])
        for i, unit in enumerate(units):
            st = st_next
            if i + 1 < len(units):
                gate_piece(i + 1)
                st_next = scores(*units[i + 1])
            attend(*unit, st)

        z = jnp.concatenate([jnp.concatenate(row, axis=1) for row in z_parts], axis=0)
        attn = jnp.concatenate([jnp.concatenate(col, axis=0) for col in out_chunks], axis=1)
        gated = (attn * _silu(z)).astype(BF16)
        for p0 in range(0, tc, OUT_ROWS):
            h2 = h[p0:p0 + OUT_ROWS, :] + _dot(gated[p0:p0 + OUT_ROWS, :], w_out_ref[:, 0:D_MODEL]) + b_out_ref[...]
            o_ref[0, r0 + p0:r0 + p0 + OUT_ROWS, :] = _rms_norm(h2, gf_ref[...])

    for c in range(tm // tc):
        chunk(c)
    kx[:, 0:WINDOW, :] = kx[:, tm:tm + WINDOW, :]
    vtx[:, 0:WINDOW] = vtx[:, tm:tm + WINDOW]


def _odd_layer(h, positions, g, w_in, b_in, sinks, w_out, b_out, g_final, *, tm, tc):
    b, s, d = h.shape
    tile = pl.BlockSpec((1, tm, d), lambda i, j: (i, j, 0))
    pos_spec = pl.BlockSpec((1, 1, tm), lambda i, j: (i, 0, j))
    weight_bytes = 2 * (w_in.size + w_out.size)
    stage_bytes = 2 * 4 * STAGE_ROWS * (w_in.shape[1] + w_out.shape[1])
    tile_bytes = 4 * tm * d
    vmem_limit = weight_bytes + stage_bytes + (4 + 12) * tile_bytes + (8 << 20)
    hbm = pl.BlockSpec(memory_space=pl.ANY)
    return pl.pallas_call(
        functools.partial(_odd_kernel, tm=tm, tc=tc),
        grid=(b, s // tm),
        in_specs=[pl.BlockSpec(memory_space=pltpu.SMEM), tile, pos_spec, _const_spec((1, d)),
                  hbm, _const_spec((1, ODD_IN)), hbm, _const_spec((1, d)), _const_spec((1, d))],
        out_specs=tile,
        out_shape=jax.ShapeDtypeStruct(h.shape, h.dtype),
        scratch_shapes=[pltpu.VMEM((2 * N_KV_HEADS, WINDOW + tm, LANES), BF16),
                        pltpu.VMEM((KV_WIDTH, WINDOW + tm), BF16),
                        pltpu.VMEM(w_in.shape, BF16),
                        pltpu.VMEM(_padded(w_out.shape), BF16),
                        pltpu.VMEM((2, STAGE_ROWS, w_in.shape[1]), F32),
                        pltpu.VMEM((2, STAGE_ROWS, w_out.shape[1]), F32),
                        pltpu.SemaphoreType.DMA((2,)), pltpu.SemaphoreType.DMA((2,))],
        compiler_params=pltpu.CompilerParams(
            dimension_semantics=("arbitrary", "arbitrary"), vmem_limit_bytes=vmem_limit),
        name="odd_layer",
    )(sinks, h, positions.reshape(b, 1, s), g.reshape(1, d), w_in, b_in.reshape(1, ODD_IN), w_out,
      b_out.reshape(1, d), g_final.reshape(1, d))


EVEN_TM = 512
EVEN_TC = 256
ODD_TM = 512
ODD_TC = 256


def kernel(x, positions, norm_g, w_in_even, w_pool, pool_scale, conv_w, w_out_even, w_in_odd, b_in_odd,
           attn_sinks, w_out_odd, b_out_odd, final_norm_g):
    assert norm_g.shape[0] == 2 and w_in_even.shape[0] == 1 and w_in_odd.shape[0] == 1
    h = _even_layer(x, norm_g[0], w_in_even[0], w_pool[0], pool_scale[0], conv_w[0], w_out_even[0],
                    tm=EVEN_TM, tc=EVEN_TC)
    return _odd_layer(h, positions, norm_g[1], w_in_odd[0], b_in_odd[0], attn_sinks[0], w_out_odd[0], b_out_odd[0],
                      final_norm_g, tm=ODD_TM, tc=ODD_TC)
```

```python
import functools
import math

import jax
import jax.numpy as jnp
from jax import lax
from jax.experimental import pallas as pl
from jax.experimental.pallas import tpu as pltpu

D_MODEL = 1024
NORM_EPS = 1e-5
POOL_WINDOWS = (2, 4, 8, 16)
POOL_GC = D_MODEL // len(POOL_WINDOWS)
CONV_WIDTH = D_MODEL
EVEN_IN = 6 * D_MODEL
HEAD_DIM = 64
N_HEADS = 16
N_KV_HEADS = 2
GROUP = N_HEADS // N_KV_HEADS
KV_WIDTH = N_KV_HEADS * HEAD_DIM
ODD_IN = 2 * D_MODEL + 2 * KV_WIDTH
WINDOW = 128
ROPE_THETA = 500000.0
ROT_DIMS = HEAD_DIM // 4
N_FREQ = ROT_DIMS // 2

VMEM_BYTES = 64 << 20
LANES = 128
SUBLANES = 8
POOL_CARRY = 32
CONV_CARRY = SUBLANES
PAIRS_PER_KV = GROUP // 2
LOG2E = math.log2(math.e)
Q_SCALE = HEAD_DIM ** -0.5 * LOG2E
MASK_BIAS = 2.0 ** 100
STAGE_ROWS = 256
OUT_ROWS = 256

_INV_FREQ = tuple(float(ROPE_THETA ** (-(2.0 * i) / ROT_DIMS)) for i in range(N_FREQ))

F32 = jnp.float32
BF16 = jnp.bfloat16


def _dot(a, b):
    return jnp.dot(a, b, preferred_element_type=F32)


def _rms_norm(x, g):
    ms = jnp.mean(x * x, axis=-1, keepdims=True)
    return x * lax.rsqrt(ms + NORM_EPS) * g


def _silu(z):
    return z * jax.nn.sigmoid(z)


def _rope_tables(pos_row, tm):
    sub = lax.broadcasted_iota(jnp.int32, (N_FREQ, tm), 0)
    inv_freq = jnp.full((N_FREQ, tm), _INV_FREQ[N_FREQ - 1], F32)
    for f in range(N_FREQ - 1):
        inv_freq = jnp.where(sub == f, _INV_FREQ[f], inv_freq)
    ang = pos_row.astype(F32) * inv_freq
    cos8 = jnp.cos(ang)
    sin8 = jnp.sin(ang)
    r = lax.broadcasted_iota(jnp.int32, (LANES, LANES), 0) & (HEAD_DIM - 1)
    c_blocks, s_blocks = [], []
    for blk in range(tm // LANES):
        c_rep = jnp.concatenate([cos8[:, blk * LANES:(blk + 1) * LANES]] * (LANES // N_FREQ), axis=0)
        s_rep = jnp.concatenate([sin8[:, blk * LANES:(blk + 1) * LANES]] * (LANES // N_FREQ), axis=0)
        c_pat = jnp.where(r < ROT_DIMS, c_rep, 1.0)
        s_pat = jnp.where(r < N_FREQ, -s_rep, jnp.where(r < ROT_DIMS, s_rep, 0.0))
        c_blocks.append(c_pat.T)
        s_blocks.append(s_pat.T)
    return jnp.concatenate(c_blocks, axis=0), jnp.concatenate(s_blocks, axis=0)


def _load_weight(src_hbm, dst, stage, sem):
    k_rows, width = src_hbm.shape
    n_blocks = k_rows // STAGE_ROWS

    def copy(k):
        return pltpu.make_async_copy(src_hbm.at[pl.ds(k * STAGE_ROWS, STAGE_ROWS), :], stage.at[k % 2], sem.at[k % 2])

    copy(0).start()
    for k in range(n_blocks):
        if k + 1 < n_blocks:
            copy(k + 1).start()
        copy(k).wait()
        dst[k * STAGE_ROWS:(k + 1) * STAGE_ROWS, 0:width] = stage[k % 2].astype(BF16)


def _first_step():
    return (pl.program_id(0) == 0) & (pl.program_id(1) == 0)


def _even_kernel(x_ref, g_ref, w_in_hbm, w_pool_hbm, ps_ref, conv_ref, w_out_hbm, o_ref,
                 ubuf, s2buf, s4buf, cbuf, w_in_ref, w_pool_ref, w_out_ref, stage_in, stage_pool, stage_out,
                 sem_in, sem_pool, sem_out, *, tm, tc):
    t = pl.program_id(1)
    n_slabs = D_MODEL // LANES
    slabs_per_group = POOL_GC // LANES
    n_s2, n_s4 = n_slabs - slabs_per_group, n_slabs - 2 * slabs_per_group
    ext = POOL_CARRY + tc

    @pl.when(_first_step())
    def _():
        _load_weight(w_in_hbm, w_in_ref, stage_in, sem_in)
        _load_weight(w_out_hbm, w_out_ref, stage_out, sem_out)
        _load_weight(w_pool_hbm, w_pool_ref, stage_pool, sem_pool)

    @pl.when(t == 0)
    def _():
        ubuf[:, 0:POOL_CARRY, :] = jnp.zeros((n_slabs, POOL_CARRY, LANES), F32)
        cbuf[:, 0:CONV_CARRY, :] = jnp.zeros((n_slabs, CONV_CARRY, LANES), F32)

    def chunk(c):
        r0 = c * tc
        x = x_ref[0, r0:r0 + tc, :]
        y = _rms_norm(x, g_ref[...]).astype(BF16)
        u = _dot(y, w_in_ref[:, 0:D_MODEL])
        gate_c = _dot(y, w_in_ref[:, 2 * D_MODEL:3 * D_MODEL])
        h_c = _dot(y, w_in_ref[:, 3 * D_MODEL:4 * D_MODEL])
        gate_b = _dot(y, w_in_ref[:, D_MODEL:2 * D_MODEL])
        z_b = _dot(y, w_in_ref[:, 5 * D_MODEL:6 * D_MODEL])
        z_a = _dot(y, w_in_ref[:, 4 * D_MODEL:5 * D_MODEL])

        pos = t * tm + r0 + lax.broadcasted_iota(jnp.int32, (tc, LANES), 0)
        a_parts = []
        for g, w in enumerate(POOL_WINDOWS):
            inv = 1.0 / jnp.minimum(pos + 1, w).astype(F32)
            slabs = []
            for j in range(g * slabs_per_group, (g + 1) * slabs_per_group):
                u_j = u[:, j * LANES:(j + 1) * LANES]
                ubuf[j, r0 + POOL_CARRY:r0 + ext, :] = u_j
                s = ubuf[j, r0 + 8:r0 + ext, :] + ubuf[j, r0 + 7:r0 + ext - 1, :]
                if w >= 4:
                    jj = c * n_s2 + j - slabs_per_group
                    s2buf[jj, :, :] = s
                    s = s[8:, :] + s2buf[jj, 6:ext - 10, :]
                if w >= 8:
                    jj = c * n_s4 + j - 2 * slabs_per_group
                    s4buf[jj, :, :] = s
                    s = s[8:, :] + s4buf[jj, 4:ext - 20, :]
                if w >= 16:
                    s = s[8:, :] + s[:-8, :]
                slabs.append(s[s.shape[0] - tc:, :] * inv - u_j)
            pooled = jnp.concatenate(slabs, axis=1).astype(BF16)
            a_parts.append(_dot(pooled, w_pool_ref[g * POOL_GC:(g + 1) * POOL_GC, :]))
        a = jnp.concatenate(a_parts, axis=1) * ps_ref[...]

        cu = gate_c * h_c
        cw = conv_ref[...]
        v_parts = []
        for j in range(n_slabs):
            lanes = slice(j * LANES, (j + 1) * LANES)
            cbuf[j, r0 + CONV_CARRY:r0 + CONV_CARRY + tc, :] = cu[:, lanes]
            v_parts.append(cw[2:3, lanes] * cu[:, lanes]
                           + cw[1:2, lanes] * cbuf[j, r0 + CONV_CARRY - 1:r0 + CONV_CARRY - 1 + tc, :]
                           + cw[0:1, lanes] * cbuf[j, r0 + CONV_CARRY - 2:r0 + CONV_CARRY - 2 + tc, :])
        bo = gate_b * jnp.concatenate(v_parts, axis=1)

        mix_b = (bo * _silu(z_b)).astype(BF16)
        mix_a = (a * _silu(z_a)).astype(BF16)
        o_ref[0, r0:r0 + tc, :] = x + (_dot(mix_b, w_out_ref[D_MODEL:, 0:D_MODEL])
                                       + _dot(mix_a, w_out_ref[0:D_MODEL, 0:D_MODEL]))

    for c in range(tm // tc):
        chunk(c)
    ubuf[:, 0:POOL_CARRY, :] = ubuf[:, tm:tm + POOL_CARRY, :]
    cbuf[:, 0:CONV_CARRY, :] = cbuf[:, tm:tm + CONV_CARRY, :]


def _const_spec(shape):
    nd = len(shape)
    return pl.BlockSpec(shape, lambda *_: (0,) * nd, pipeline_mode=pl.Buffered(1))


def _padded(shape):
    k_rows, width = shape
    return (k_rows, width + LANES) if (width // LANES) % 2 == 0 else (k_rows, width)


def _even_layer(x, g, w_in, w_pool, pool_scale, conv_w, w_out, *, tm, tc):
    b, s, d = x.shape
    tile = pl.BlockSpec((1, tm, d), lambda i, j: (i, j, 0))
    n_slabs, per_group = d // LANES, POOL_GC // LANES
    w_pool = w_pool.reshape(-1, w_pool.shape[-1])
    weight_bytes = 2 * (w_in.size + w_pool.size + w_out.size)
    stage_bytes = 2 * 4 * STAGE_ROWS * (w_in.shape[1] + w_pool.shape[1] + w_out.shape[1])
    tile_bytes, chunk_bytes = 4 * tm * d, 4 * tc * d
    assert weight_bytes + stage_bytes + (4 + 2) * tile_bytes + (3 + 10) * chunk_bytes <= VMEM_BYTES
    vmem_limit = VMEM_BYTES
    hbm = pl.BlockSpec(memory_space=pl.ANY)
    return pl.pallas_call(
        functools.partial(_even_kernel, tm=tm, tc=tc),
        grid=(b, s // tm),
        in_specs=[tile, _const_spec((1, d)), hbm, hbm, _const_spec((1, d)), _const_spec(conv_w.shape), hbm],
        out_specs=tile,
        out_shape=jax.ShapeDtypeStruct(x.shape, x.dtype),
        scratch_shapes=[pltpu.VMEM((n_slabs, POOL_CARRY + tm, LANES), F32),
                        pltpu.VMEM((tm // tc * (n_slabs - per_group), POOL_CARRY + tc - 8, LANES), F32),
                        pltpu.VMEM((tm // tc * (n_slabs - 2 * per_group), POOL_CARRY + tc - 16, LANES), F32),
                        pltpu.VMEM((n_slabs, CONV_CARRY + tm, LANES), F32),
                        pltpu.VMEM(_padded(w_in.shape), BF16),
                        pltpu.VMEM(w_pool.shape, BF16),
                        pltpu.VMEM(_padded(w_out.shape), BF16),
                        pltpu.VMEM((2, STAGE_ROWS, w_in.shape[1]), F32),
                        pltpu.VMEM((2, STAGE_ROWS, w_pool.shape[1]), F32),
                        pltpu.VMEM((2, STAGE_ROWS, w_out.shape[1]), F32),
                        pltpu.SemaphoreType.DMA((2,)), pltpu.SemaphoreType.DMA((2,)), pltpu.SemaphoreType.DMA((2,))],
        compiler_params=pltpu.CompilerParams(
            dimension_semantics=("arbitrary", "arbitrary"), vmem_limit_bytes=vmem_limit),
        name="even_layer",
    )(x, g.reshape(1, d), w_in, w_pool, pool_scale.reshape(1, d), conv_w, w_out)


def _odd_kernel(sink_ref, h_ref, pos_ref, g_ref, w_in_hbm, b_in_ref, w_out_hbm, b_out_ref, gf_ref,
                o_ref, kx, vtx, w_in_ref, w_out_ref, stage_in, stage_out, sem_in, sem_out, *, tm, tc):
    t = pl.program_id(1)
    nq = tc // WINDOW
    qkv_w = D_MODEL + 2 * KV_WIDTH

    @pl.when(_first_step())
    def _():
        _load_weight(w_in_hbm, w_in_ref, stage_in, sem_in)
        _load_weight(w_out_hbm, w_out_ref, stage_out, sem_out)

    @pl.when(t == 0)
    def _():
        kx[:, 0:WINDOW, :] = jnp.zeros((2 * N_KV_HEADS, WINDOW, LANES), BF16)
        vtx[:, 0:WINDOW] = jnp.zeros((KV_WIDTH, WINDOW), BF16)

    lane = lax.broadcasted_iota(jnp.int32, (tc, LANES), 1)
    first_half = (lane & (HEAD_DIM - 1)) < N_FREQ
    head0 = lane < HEAD_DIM

    def rope(xc, cc, sc):
        swapped = jnp.where(first_half, pltpu.roll(xc, LANES - N_FREQ, 1), pltpu.roll(xc, N_FREQ, 1))
        return xc * cc + swapped * sc

    slot = lax.broadcasted_iota(jnp.int32, (WINDOW, LANES), 0)
    qidx = lax.broadcasted_iota(jnp.int32, (WINDOW, LANES), 1)
    diff = slot - qidx
    eye = jnp.where(diff == 0, 1.0, 0.0).astype(BF16)
    masked = jnp.full((WINDOW, LANES), -MASK_BIAS, F32)
    bias_prev = jnp.where(diff > 0, 0.0, masked)
    bias_cur = jnp.where(diff <= 0, 0.0, masked)
    bias_mid = jnp.concatenate([bias_prev, bias_cur], axis=0).astype(BF16)
    bias_first = jnp.concatenate([jnp.where(t == 0, masked, bias_prev), bias_cur], axis=0).astype(BF16)
    ones_rows = jnp.ones((2 * SUBLANES, 2 * WINDOW), BF16)

    def chunk(c):
        r0 = c * tc
        h = h_ref[0, r0:r0 + tc, :]
        y = _rms_norm(h, g_ref[...]).astype(BF16)
        kv_p = _dot(y, w_in_ref[:, D_MODEL:qkv_w]) + b_in_ref[:, D_MODEL:qkv_w]
        proj = _dot(y, w_in_ref[:, 0:D_MODEL]) + b_in_ref[:, 0:D_MODEL]
        k = kv_p[:, 0:KV_WIDTH]
        v = kv_p[:, KV_WIDTH:]

        c_tab, s_tab = _rope_tables(pos_ref[0, :, r0:r0 + tc], tc)
        cq = c_tab * Q_SCALE
        sq = s_tab * Q_SCALE
        qr = [rope(proj[:, j * LANES:(j + 1) * LANES], cq, sq).astype(BF16) for j in range(D_MODEL // LANES)]
        kr = rope(k, c_tab, s_tab)

        k0, k1 = WINDOW + r0, WINDOW + r0 + tc
        lo0 = jnp.where(head0, kr, 0.0)
        hi1 = jnp.where(head0, 0.0, kr)
        kx[0, k0:k1, :] = lo0.astype(BF16)
        kx[1, k0:k1, :] = pltpu.roll(lo0, HEAD_DIM, 1).astype(BF16)
        kx[2, k0:k1, :] = pltpu.roll(hi1, HEAD_DIM, 1).astype(BF16)
        kx[3, k0:k1, :] = hi1.astype(BF16)
        for blk in range(nq):
            vtx[:, k0 + blk * WINDOW:k0 + (blk + 1) * WINDOW] = v[blk * WINDOW:(blk + 1) * WINDOW, :].T.astype(BF16)

        out_chunks = [[None] * nq for _ in range(D_MODEL // LANES)]
        units = [(qb, kv) for qb in range(nq) for kv in range(N_KV_HEADS)]

        def scores(qb, kv):
            a0 = r0 + qb * WINDOW
            bias = bias_first if a0 == 0 else bias_mid
            w_ext = jnp.concatenate(
                [jnp.concatenate([kx[2 * kv + par, a0:a0 + 2 * WINDOW, :], bias], axis=1) for par in range(2)],
                axis=0)
            q_ext = jnp.concatenate(
                [jnp.concatenate([qr[kv * PAIRS_PER_KV + j][qb * WINDOW:(qb + 1) * WINDOW, :], eye], axis=1)
                 for j in range(PAIRS_PER_KV)], axis=0)
            return lax.dot_general(w_ext, q_ext, (((1,), (1,)), ((), ())), preferred_element_type=F32)

        def attend(qb, kv, st):
            a0 = r0 + qb * WINDOW
            blocks = []
            terms = []
            for j in range(PAIRS_PER_KV):
                for par in range(2):
                    sink = sink_ref[kv * GROUP + 2 * j + par] * LOG2E
                    blk_s = st[par * 2 * WINDOW:(par + 1) * 2 * WINDOW, j * WINDOW:(j + 1) * WINDOW]
                    m = jnp.maximum(jnp.max(blk_s, axis=0, keepdims=True), sink)
                    blocks.append(jnp.exp2(blk_s - m).astype(BF16))
                    terms.append(jnp.exp2(sink - m))
            p_t = jnp.concatenate(blocks, axis=1)
            v_win = vtx[kv * HEAD_DIM:(kv + 1) * HEAD_DIM, a0:a0 + 2 * WINDOW]
            o_t = _dot(jnp.concatenate([v_win, ones_rows], axis=0), p_t)
            den = o_t[HEAD_DIM:HEAD_DIM + SUBLANES, :] + jnp.concatenate(terms, axis=1)
            n_t = o_t[:HEAD_DIM, :] * jnp.concatenate([1.0 / den] * (HEAD_DIM // SUBLANES), axis=0)
            for j in range(PAIRS_PER_KV):
                pair = jnp.concatenate([n_t[:, 2 * j * WINDOW:(2 * j + 1) * WINDOW],
                                        n_t[:, (2 * j + 1) * WINDOW:(2 * j + 2) * WINDOW]], axis=0)
                out_chunks[kv * PAIRS_PER_KV + j][qb] = pair.T

        z_cols = max(2 * LANES, D_MODEL // len(units))
        n_zc = D_MODEL // z_cols
        z_rows = tc * n_zc // len(units)
        z_parts = [[None] * n_zc for _ in range(tc // z_rows)]

        def gate_piece(i):
            z0, c0 = (i // n_zc) * z_rows, qkv_w + (i % n_zc) * z_cols
            z_parts[i // n_zc][i % n_zc] = (_dot(y[z0:z0 + z_rows, :], w_in_ref[:, c0:c0 + z_cols])
                                            + b_in_ref[:, c0:c0 + z_cols])

        gate_piece(0)
        st_next = scores(*units[0])
        for i, unit in enumerate(units):
            st = st_next
            if i + 1 < len(units):
                gate_piece(i + 1)
                st_next = scores(*units[i + 1])
            attend(*unit, st)

        z = jnp.concatenate([jnp.concatenate(row, axis=1) for row in z_parts], axis=0)
        attn = jnp.concatenate([jnp.concatenate(col, axis=0) for col in out_chunks], axis=1)
        gated = (attn * _silu(z)).astype(BF16)
        for p0 in range(0, tc, OUT_ROWS):
            h2 = h[p0:p0 + OUT_ROWS, :] + _dot(gated[p0:p0 + OUT_ROWS, :], w_out_ref[:, 0:D_MODEL]) + b_out_ref[...]
            o_ref[0, r0 + p0:r0 + p0 + OUT_ROWS, :] = _rms_norm(h2, gf_ref[...])

    for c in range(tm // tc):
        chunk(c)
    kx[:, 0:WINDOW, :] = kx[:, tm:tm + WINDOW, :]
    vtx[:, 0:WINDOW] = vtx[:, tm:tm + WINDOW]


def _odd_layer(h, positions, g, w_in, b_in, sinks, w_out, b_out, g_final, *, tm, tc):
    b, s, d = h.shape
    tile = pl.BlockSpec((1, tm, d), lambda i, j: (i, j, 0))
    pos_spec = pl.BlockSpec((1, 1, tm), lambda i, j: (i, 0, j))
    weight_bytes = 2 * (w_in.size + w_out.size)
    stage_bytes = 2 * 4 * STAGE_ROWS * (w_in.shape[1] + w_out.shape[1])
    tile_bytes = 4 * tm * d
    vmem_limit = weight_bytes + stage_bytes + (4 + 12) * tile_bytes + (8 << 20)
    hbm = pl.BlockSpec(memory_space=pl.ANY)
    return pl.pallas_call(
        functools.partial(_odd_kernel, tm=tm, tc=tc),
        grid=(b, s // tm),
        in_specs=[pl.BlockSpec(memory_space=pltpu.SMEM), tile, pos_spec, _const_spec((1, d)),
                  hbm, _const_spec((1, ODD_IN)), hbm, _const_spec((1, d)), _const_spec((1, d))],
        out_specs=tile,
        out_shape=jax.ShapeDtypeStruct(h.shape, h.dtype),
        scratch_shapes=[pltpu.VMEM((2 * N_KV_HEADS, WINDOW + tm, LANES), BF16),
                        pltpu.VMEM((KV_WIDTH, WINDOW + tm), BF16),
                        pltpu.VMEM(w_in.shape, BF16),
                        pltpu.VMEM(_padded(w_out.shape), BF16),
                        pltpu.VMEM((2, STAGE_ROWS, w_in.shape[1]), F32),
                        pltpu.VMEM((2, STAGE_ROWS, w_out.shape[1]), F32),
                        pltpu.SemaphoreType.DMA((2,)), pltpu.SemaphoreType.DMA((2,))],
        compiler_params=pltpu.CompilerParams(
            dimension_semantics=("arbitrary", "arbitrary"), vmem_limit_bytes=vmem_limit),
        name="odd_layer",
    )(sinks, h, positions.reshape(b, 1, s), g.reshape(1, d), w_in, b_in.reshape(1, ODD_IN), w_out,
      b_out.reshape(1, d), g_final.reshape(1, d))


EVEN_TM = 512
EVEN_TC = 256
ODD_TM = 512
ODD_TC = 256


def kernel(x, positions, norm_g, w_in_even, w_pool, pool_scale, conv_w, w_out_even, w_in_odd, b_in_odd,
           attn_sinks, w_out_odd, b_out_odd, final_norm_g):
    assert norm_g.shape[0] == 2 and w_in_even.shape[0] == 1 and w_in_odd.shape[0] == 1
    h = _even_layer(x, norm_g[0], w_in_even[0], w_pool[0], pool_scale[0], conv_w[0], w_out_even[0],
                    tm=EVEN_TM, tc=EVEN_TC)
    return _odd_layer(h, positions, norm_g[1], w_in_odd[0], b_in_odd[0], attn_sinks[0], w_out_odd[0], b_out_odd[0],
                      final_norm_g, tm=ODD_TM, tc=ODD_TC)
```

```python
import functools
import math

import jax
import jax.numpy as jnp
from jax import lax
from jax.experimental import pallas as pl
from jax.experimental.pallas import tpu as pltpu

D_MODEL = 1024
NORM_EPS = 1e-5
POOL_WINDOWS = (2, 4, 8, 16)
POOL_GC = D_MODEL // len(POOL_WINDOWS)
CONV_WIDTH = D_MODEL
EVEN_IN = 6 * D_MODEL
HEAD_DIM = 64
N_HEADS = 16
N_KV_HEADS = 2
GROUP = N_HEADS // N_KV_HEADS
KV_WIDTH = N_KV_HEADS * HEAD_DIM
ODD_IN = 2 * D_MODEL + 2 * KV_WIDTH
WINDOW = 128
ROPE_THETA = 500000.0
ROT_DIMS = HEAD_DIM // 4
N_FREQ = ROT_DIMS // 2

VMEM_BYTES = 64 << 20
LANES = 128
SUBLANES = 8
POOL_CARRY = 32
CONV_CARRY = SUBLANES
PAIRS_PER_KV = GROUP // 2
LOG2E = math.log2(math.e)
Q_SCALE = HEAD_DIM ** -0.5 * LOG2E
MASK_BIAS = 2.0 ** 100
STAGE_ROWS = 256
OUT_ROWS = 256

_INV_FREQ = tuple(float(ROPE_THETA ** (-(2.0 * i) / ROT_DIMS)) for i in range(N_FREQ))

F32 = jnp.float32
BF16 = jnp.bfloat16


def _dot(a, b):
    return jnp.dot(a, b, preferred_element_type=F32)


def _rms_norm(x, g):
    ms = jnp.mean(x * x, axis=-1, keepdims=True)
    return x * lax.rsqrt(ms + NORM_EPS) * g


def _silu(z):
    return z * jax.nn.sigmoid(z)


def _rope_tables(pos_row, tm):
    sub = lax.broadcasted_iota(jnp.int32, (N_FREQ, tm), 0)
    inv_freq = jnp.full((N_FREQ, tm), _INV_FREQ[N_FREQ - 1], F32)
    for f in range(N_FREQ - 1):
        inv_freq = jnp.where(sub == f, _INV_FREQ[f], inv_freq)
    ang = pos_row.astype(F32) * inv_freq
    cos8 = jnp.cos(ang)
    sin8 = jnp.sin(ang)
    r = lax.broadcasted_iota(jnp.int32, (LANES, LANES), 0) & (HEAD_DIM - 1)
    c_blocks, s_blocks = [], []
    for blk in range(tm // LANES):
        c_rep = jnp.concatenate([cos8[:, blk * LANES:(blk + 1) * LANES]] * (LANES // N_FREQ), axis=0)
        s_rep = jnp.concatenate([sin8[:, blk * LANES:(blk + 1) * LANES]] * (LANES // N_FREQ), axis=0)
        c_pat = jnp.where(r < ROT_DIMS, c_rep, 1.0)
        s_pat = jnp.where(r < N_FREQ, -s_rep, jnp.where(r < ROT_DIMS, s_rep, 0.0))
        c_blocks.append(c_pat.T)
        s_blocks.append(s_pat.T)
    return jnp.concatenate(c_blocks, axis=0), jnp.concatenate(s_blocks, axis=0)


def _load_weight(src_hbm, dst, stage, sem):
    k_rows, width = src_hbm.shape
    n_blocks = k_rows // STAGE_ROWS

    def copy(k):
        return pltpu.make_async_copy(src_hbm.at[pl.ds(k * STAGE_ROWS, STAGE_ROWS), :], stage.at[k % 2], sem.at[k % 2])

    copy(0).start()
    for k in range(n_blocks):
        if k + 1 < n_blocks:
            copy(k + 1).start()
        copy(k).wait()
        dst[k * STAGE_ROWS:(k + 1) * STAGE_ROWS, 0:width] = stage[k % 2].astype(BF16)


def _first_step():
    return (pl.program_id(0) == 0) & (pl.program_id(1) == 0)


def _even_kernel(x_ref, g_ref, w_in_hbm, w_pool_hbm, ps_ref, conv_ref, w_out_hbm, o_ref,
                 ubuf, s2buf, s4buf, cbuf, w_in_ref, w_pool_ref, w_out_ref, stage_in, stage_pool, stage_out,
                 sem_in, sem_pool, sem_out, *, tm, tc):
    t = pl.program_id(1)
    n_slabs = D_MODEL // LANES
    slabs_per_group = POOL_GC // LANES
    n_s2, n_s4 = n_slabs - slabs_per_group, n_slabs - 2 * slabs_per_group
    ext = POOL_CARRY + tc

    @pl.when(_first_step())
    def _():
        _load_weight(w_in_hbm, w_in_ref, stage_in, sem_in)
        _load_weight(w_out_hbm, w_out_ref, stage_out, sem_out)
        _load_weight(w_pool_hbm, w_pool_ref, stage_pool, sem_pool)

    @pl.when(t == 0)
    def _():
        ubuf[:, 0:POOL_CARRY, :] = jnp.zeros((n_slabs, POOL_CARRY, LANES), F32)
        cbuf[:, 0:CONV_CARRY, :] = jnp.zeros((n_slabs, CONV_CARRY, LANES), F32)

    def chunk(c):
        r0 = c * tc
        x = x_ref[0, r0:r0 + tc, :]
        y = _rms_norm(x, g_ref[...]).astype(BF16)
        u = _dot(y, w_in_ref[:, 0:D_MODEL])
        gate_c = _dot(y, w_in_ref[:, 2 * D_MODEL:3 * D_MODEL])
        h_c = _dot(y, w_in_ref[:, 3 * D_MODEL:4 * D_MODEL])
        gate_b = _dot(y, w_in_ref[:, D_MODEL:2 * D_MODEL])
        z_b = _dot(y, w_in_ref[:, 5 * D_MODEL:6 * D_MODEL])
        z_a = _dot(y, w_in_ref[:, 4 * D_MODEL:5 * D_MODEL])

        pos = t * tm + r0 + lax.broadcasted_iota(jnp.int32, (tc, LANES), 0)
        a_parts = []
        for g, w in enumerate(POOL_WINDOWS):
            inv = 1.0 / jnp.minimum(pos + 1, w).astype(F32)
            slabs = []
            for j in range(g * slabs_per_group, (g + 1) * slabs_per_group):
                u_j = u[:, j * LANES:(j + 1) * LANES]
                ubuf[j, r0 + POOL_CARRY:r0 + ext, :] = u_j
                s = ubuf[j, r0 + 8:r0 + ext, :] + ubuf[j, r0 + 7:r0 + ext - 1, :]
                if w >= 4:
                    jj = c * n_s2 + j - slabs_per_group
                    s2buf[jj, :, :] = s
                    s = s[8:, :] + s2buf[jj, 6:ext - 10, :]
                if w >= 8:
                    jj = c * n_s4 + j - 2 * slabs_per_group
                    s4buf[jj, :, :] = s
                    s = s[8:, :] + s4buf[jj, 4:ext - 20, :]
                if w >= 16:
                    s = s[8:, :] + s[:-8, :]
                slabs.append(s[s.shape[0] - tc:, :] * inv - u_j)
            pooled = jnp.concatenate(slabs, axis=1).astype(BF16)
            a_parts.append(_dot(pooled, w_pool_ref[g * POOL_GC:(g + 1) * POOL_GC, :]))
        a = jnp.concatenate(a_parts, axis=1) * ps_ref[...]

        cu = gate_c * h_c
        cw = conv_ref[...]
        v_parts = []
        for j in range(n_slabs):
            lanes = slice(j * LANES, (j + 1) * LANES)
            cbuf[j, r0 + CONV_CARRY:r0 + CONV_CARRY + tc, :] = cu[:, lanes]
            v_parts.append(cw[2:3, lanes] * cu[:, lanes]
                           + cw[1:2, lanes] * cbuf[j, r0 + CONV_CARRY - 1:r0 + CONV_CARRY - 1 + tc, :]
                           + cw[0:1, lanes] * cbuf[j, r0 + CONV_CARRY - 2:r0 + CONV_CARRY - 2 + tc, :])
        bo = gate_b * jnp.concatenate(v_parts, axis=1)

        mix_b = (bo * _silu(z_b)).astype(BF16)
        mix_a = (a * _silu(z_a)).astype(BF16)
        o_ref[0, r0:r0 + tc, :] = x + (_dot(mix_b, w_out_ref[D_MODEL:, 0:D_MODEL])
                                       + _dot(mix_a, w_out_ref[0:D_MODEL, 0:D_MODEL]))

    for c in range(tm // tc):
        chunk(c)
    ubuf[:, 0:POOL_CARRY, :] = ubuf[:, tm:tm + POOL_CARRY, :]
    cbuf[:, 0:CONV_CARRY, :] = cbuf[:, tm:tm + CONV_CARRY, :]


def _const_spec(shape):
    nd = len(shape)
    return pl.BlockSpec(shape, lambda *_: (0,) * nd, pipeline_mode=pl.Buffered(1))


def _padded(shape):
    k_rows, width = shape
    return (k_rows, width + LANES) if (width // LANES) % 2 == 0 else (k_rows, width)


def _even_layer(x, g, w_in, w_pool, pool_scale, conv_w, w_out, *, tm, tc):
    b, s, d = x.shape
    tile = pl.BlockSpec((1, tm, d), lambda i, j: (i, j, 0))
    n_slabs, per_group = d // LANES, POOL_GC // LANES
    w_pool = w_pool.reshape(-1, w_pool.shape[-1])
    weight_bytes = 2 * (w_in.size + w_pool.size + w_out.size)
    stage_bytes = 2 * 4 * STAGE_ROWS * (w_in.shape[1] + w_pool.shape[1] + w_out.shape[1])
    tile_bytes, chunk_bytes = 4 * tm * d, 4 * tc * d
    assert weight_bytes + stage_bytes + (4 + 2) * tile_bytes + (3 + 10) * chunk_bytes <= VMEM_BYTES
    vmem_limit = VMEM_BYTES
    hbm = pl.BlockSpec(memory_space=pl.ANY)
    return pl.pallas_call(
        functools.partial(_even_kernel, tm=tm, tc=tc),
        grid=(b, s // tm),
        in_specs=[tile, _const_spec((1, d)), hbm, hbm, _const_spec((1, d)), _const_spec(conv_w.shape), hbm],
        out_specs=tile,
        out_shape=jax.ShapeDtypeStruct(x.shape, x.dtype),
        scratch_shapes=[pltpu.VMEM((n_slabs, POOL_CARRY + tm, LANES), F32),
                        pltpu.VMEM((tm // tc * (n_slabs - per_group), POOL_CARRY + tc - 8, LANES), F32),
                        pltpu.VMEM((tm // tc * (n_slabs - 2 * per_group), POOL_CARRY + tc - 16, LANES), F32),
                        pltpu.VMEM((n_slabs, CONV_CARRY + tm, LANES), F32),
                        pltpu.VMEM(_padded(w_in.shape), BF16),
                        pltpu.VMEM(w_pool.shape, BF16),
                        pltpu.VMEM(_padded(w_out.shape), BF16),
                        pltpu.VMEM((2, STAGE_ROWS, w_in.shape[1]), F32),
                        pltpu.VMEM((2, STAGE_ROWS, w_pool.shape[1]), F32),
                        pltpu.VMEM((2, STAGE_ROWS, w_out.shape[1]), F32),
                        pltpu.SemaphoreType.DMA((2,)), pltpu.SemaphoreType.DMA((2,)), pltpu.SemaphoreType.DMA((2,))],
        compiler_params=pltpu.CompilerParams(
            dimension_semantics=("arbitrary", "arbitrary"), vmem_limit_bytes=vmem_limit),
        name="even_layer",
    )(x, g.reshape(1, d), w_in, w_pool, pool_scale.reshape(1, d), conv_w, w_out)


def _odd_kernel(sink_ref, h_ref, pos_ref, g_ref, w_in_hbm, b_in_ref, w_out_hbm, b_out_ref, gf_ref,
                o_ref, kx, vtx, w_in_ref, w_out_ref, stage_in, stage_out, sem_in, sem_out, *, tm, tc):
    t = pl.program_id(1)
    nq = tc // WINDOW
    qkv_w = D_MODEL + 2 * KV_WIDTH

    @pl.when(_first_step())
    def _():
        _load_weight(w_in_hbm, w_in_ref, stage_in, sem_in)
        _load_weight(w_out_hbm, w_out_ref, stage_out, sem_out)

    @pl.when(t == 0)
    def _():
        kx[:, 0:WINDOW, :] = jnp.zeros((2 * N_KV_HEADS, WINDOW, LANES), BF16)
        vtx[:, 0:WINDOW] = jnp.zeros((KV_WIDTH, WINDOW), BF16)

    lane = lax.broadcasted_iota(jnp.int32, (tc, LANES), 1)
    first_half = (lane & (HEAD_DIM - 1)) < N_FREQ
    head0 = lane < HEAD_DIM

    def rope(xc, cc, sc):
        swapped = jnp.where(first_half, pltpu.roll(xc, LANES - N_FREQ, 1), pltpu.roll(xc, N_FREQ, 1))
        return xc * cc + swapped * sc

    slot = lax.broadcasted_iota(jnp.int32, (WINDOW, LANES), 0)
    qidx = lax.broadcasted_iota(jnp.int32, (WINDOW, LANES), 1)
    diff = slot - qidx
    eye = jnp.where(diff == 0, 1.0, 0.0).astype(BF16)
    masked = jnp.full((WINDOW, LANES), -MASK_BIAS, F32)
    bias_prev = jnp.where(diff > 0, 0.0, masked)
    bias_cur = jnp.where(diff <= 0, 0.0, masked)
    bias_mid = jnp.concatenate([bias_prev, bias_cur], axis=0).astype(BF16)
    bias_first = jnp.concatenate([jnp.where(t == 0, masked, bias_prev), bias_cur], axis=0).astype(BF16)
    ones_rows = jnp.ones((2 * SUBLANES, 2 * WINDOW), BF16)

    def chunk(c):
        r0 = c * tc
        h = h_ref[0, r0:r0 + tc, :]
        y = _rms_norm(h, g_ref[...]).astype(BF16)
        kv_p = _dot(y, w_in_ref[:, D_MODEL:qkv_w]) + b_in_ref[:, D_MODEL:qkv_w]
        proj = _dot(y, w_in_ref[:, 0:D_MODEL]) + b_in_ref[:, 0:D_MODEL]
        k = kv_p[:, 0:KV_WIDTH]
        v = kv_p[:, KV_WIDTH:]

        c_tab, s_tab = _rope_tables(pos_ref[0, :, r0:r0 + tc], tc)
        cq = c_tab * Q_SCALE
        sq = s_tab * Q_SCALE
        qr = [rope(proj[:, j * LANES:(j + 1) * LANES], cq, sq).astype(BF16) for j in range(D_MODEL // LANES)]
        kr = rope(k, c_tab, s_tab)

        k0, k1 = WINDOW + r0, WINDOW + r0 + tc
        lo0 = jnp.where(head0, kr, 0.0)
        hi1 = jnp.where(head0, 0.0, kr)
        kx[0, k0:k1, :] = lo0.astype(BF16)
        kx[1, k0:k1, :] = pltpu.roll(lo0, HEAD_DIM, 1).astype(BF16)
        kx[2, k0:k1, :] = pltpu.roll(hi1, HEAD_DIM, 1).astype(BF16)
        kx[3, k0:k1, :] = hi1.astype(BF16)
        for blk in range(nq):
            vtx[:, k0 + blk * WINDOW:k0 + (blk + 1) * WINDOW] = v[blk * WINDOW:(blk + 1) * WINDOW, :].T.astype(BF16)

        out_chunks = [[None] * nq for _ in range(D_MODEL // LANES)]
        units = [(qb, kv) for qb in range(nq) for kv in range(N_KV_HEADS)]

        def scores(qb, kv):
            a0 = r0 + qb * WINDOW
            bias = bias_first if a0 == 0 else bias_mid
            w_ext = jnp.concatenate(
                [jnp.concatenate([kx[2 * kv + par, a0:a0 + 2 * WINDOW, :], bias], axis=1) for par in range(2)],
                axis=0)
            q_ext = jnp.concatenate(
                [jnp.concatenate([qr[kv * PAIRS_PER_KV + j][qb * WINDOW:(qb + 1) * WINDOW, :], eye], axis=1)
                 for j in range(PAIRS_PER_KV)], axis=0)
            return lax.dot_general(w_ext, q_ext, (((1,), (1,)), ((), ())), preferred_element_type=F32)

        def attend(qb, kv, st):
            a0 = r0 + qb * WINDOW
            blocks = []
            terms = []
            for j in range(PAIRS_PER_KV):
                for par in range(2):
                    sink = sink_ref[kv * GROUP + 2 * j + par] * LOG2E
                    blk_s = st[par * 2 * WINDOW:(par + 1) * 2 * WINDOW, j * WINDOW:(j + 1) * WINDOW]
                    m = jnp.maximum(jnp.max(blk_s, axis=0, keepdims=True), sink)
                    blocks.append(jnp.exp2(blk_s - m).astype(BF16))
                    terms.append(jnp.exp2(sink - m))
            p_t = jnp.concatenate(blocks, axis=1)
            v_win = vtx[kv * HEAD_DIM:(kv + 1) * HEAD_DIM, a0:a0 + 2 * WINDOW]
            o_t = _dot(jnp.concatenate([v_win, ones_rows], axis=0), p_t)
            den = o_t[HEAD_DIM:HEAD_DIM + SUBLANES, :] + jnp.concatenate(terms, axis=1)
            n_t = o_t[:HEAD_DIM, :] * jnp.concatenate([1.0 / den] * (HEAD_DIM // SUBLANES), axis=0)
            for j in range(PAIRS_PER_KV):
                pair = jnp.concatenate([n_t[:, 2 * j * WINDOW:(2 * j + 1) * WINDOW],
                                        n_t[:, (2 * j + 1) * WINDOW:(2 * j + 2) * WINDOW]], axis=0)
                out_chunks[kv * PAIRS_PER_KV + j][qb] = pair.T

        z_cols = max(2 * LANES, D_MODEL // len(units))
        n_zc = D_MODEL // z_cols
        z_rows = tc * n_zc // len(units)
        z_parts = [[None] * n_zc for _ in range(tc // z_rows)]

        def gate_piece(i):
            z0, c0 = (i // n_zc) * z_rows, qkv_w + (i % n_zc) * z_cols
            z_parts[i // n_zc][i % n_zc] = (_dot(y[z0:z0 + z_rows, :], w_in_ref[:, c0:c0 + z_cols])
                                            + b_in_ref[:, c0:c0 + z_cols])

        gate_piece(0)
        st_next = scores(*units[0])
        for i, unit in enumerate(units):
            st = st_next
            if i + 1 < len(units):
                gate_piece(i + 1)
                st_next = scores(*units[i + 1])
            attend(*unit, st)

        z = jnp.concatenate([jnp.concatenate(row, axis=1) for row in z_parts], axis=0)
        attn = jnp.concatenate([jnp.concatenate(col, axis=0) for col in out_chunks], axis=1)
        gated = (attn * _silu(z)).astype(BF16)
        for p0 in range(0, tc, OUT_ROWS):
            h2 = h[p0:p0 + OUT_ROWS, :] + _dot(gated[p0:p0 + OUT_ROWS, :], w_out_ref[:, 0:D_MODEL]) + b_out_ref[...]
            o_ref[0, r0 + p0:r0 + p0 + OUT_ROWS, :] = _rms_norm(h2, gf_ref[...])

    for c in range(tm // tc):
        chunk(c)
    kx[:, 0:WINDOW, :] = kx[:, tm:tm + WINDOW, :]
    vtx[:, 0:WINDOW] = vtx[:, tm:tm + WINDOW]


def _odd_layer(h, positions, g, w_in, b_in, sinks, w_out, b_out, g_final, *, tm, tc):
    b, s, d = h.shape
    tile = pl.BlockSpec((1, tm, d), lambda i, j: (i, j, 0))
    pos_spec = pl.BlockSpec((1, 1, tm), lambda i, j: (i, 0, j))
    weight_bytes = 2 * (w_in.size + w_out.size)
    stage_bytes = 2 * 4 * STAGE_ROWS * (w_in.shape[1] + w_out.shape[1])
    tile_bytes = 4 * tm * d
    vmem_limit = weight_bytes + stage_bytes + (4 + 12) * tile_bytes + (8 << 20)
    hbm = pl.BlockSpec(memory_space=pl.ANY)
    return pl.pallas_call(
        functools.partial(_odd_kernel, tm=tm, tc=tc),
        grid=(b, s // tm),
        in_specs=[pl.BlockSpec(memory_space=pltpu.SMEM), tile, pos_spec, _const_spec((1, d)),
                  hbm, _const_spec((1, ODD_IN)), hbm, _const_spec((1, d)), _const_spec((1, d))],
        out_specs=tile,
        out_shape=jax.ShapeDtypeStruct(h.shape, h.dtype),
        scratch_shapes=[pltpu.VMEM((2 * N_KV_HEADS, WINDOW + tm, LANES), BF16),
                        pltpu.VMEM((KV_WIDTH, WINDOW + tm), BF16),
                        pltpu.VMEM(w_in.shape, BF16),
                        pltpu.VMEM(_padded(w_out.shape), BF16),
                        pltpu.VMEM((2, STAGE_ROWS, w_in.shape[1]), F32),
                        pltpu.VMEM((2, STAGE_ROWS, w_out.shape[1]), F32),
                        pltpu.SemaphoreType.DMA((2,)), pltpu.SemaphoreType.DMA((2,))],
        compiler_params=pltpu.CompilerParams(
            dimension_semantics=("arbitrary", "arbitrary"), vmem_limit_bytes=vmem_limit),
        name="odd_layer",
    )(sinks, h, positions.reshape(b, 1, s), g.reshape(1, d), w_in, b_in.reshape(1, ODD_IN), w_out,
      b_out.reshape(1, d), g_final.reshape(1, d))


EVEN_TM = 512
EVEN_TC = 256
ODD_TM = 1024
ODD_TC = 256


def kernel(x, positions, norm_g, w_in_even, w_pool, pool_scale, conv_w, w_out_even, w_in_odd, b_in_odd,
           attn_sinks, w_out_odd, b_out_odd, final_norm_g):
    assert norm_g.shape[0] == 2 and w_in_even.shape[0] == 1 and w_in_odd.shape[0] == 1
    h = _even_layer(x, norm_g[0], w_in_even[0], w_pool[0], pool_scale[0], conv_w[0], w_out_even[0],
                    tm=EVEN_TM, tc=EVEN_TC)
    return _odd_layer(h, positions, norm_g[1], w_in_odd[0], b_in_odd[0], attn_sinks[0], w_out_odd[0], b_out_odd[0],
                      final_norm_g, tm=ODD_TM, tc=ODD_TC)
```

```python
import functools
import math

import jax
import jax.numpy as jnp
from jax import lax
from jax.experimental import pallas as pl
from jax.experimental.pallas import tpu as pltpu

D_MODEL = 1024
NORM_EPS = 1e-5
POOL_WINDOWS = (2, 4, 8, 16)
POOL_GC = D_MODEL // len(POOL_WINDOWS)
CONV_WIDTH = D_MODEL
EVEN_IN = 6 * D_MODEL
HEAD_DIM = 64
N_HEADS = 16
N_KV_HEADS = 2
GROUP = N_HEADS // N_KV_HEADS
KV_WIDTH = N_KV_HEADS * HEAD_DIM
ODD_IN = 2 * D_MODEL + 2 * KV_WIDTH
WINDOW = 128
ROPE_THETA = 500000.0
ROT_DIMS = HEAD_DIM // 4
N_FREQ = ROT_DIMS // 2

VMEM_BYTES = 64 << 20
LANES = 128
SUBLANES = 8
POOL_CARRY = 32
CONV_CARRY = SUBLANES
PAIRS_PER_KV = GROUP // 2
LOG2E = math.log2(math.e)
Q_SCALE = HEAD_DIM ** -0.5 * LOG2E
MASK_BIAS = 2.0 ** 100
STAGE_SLOTS = 4
STAGE_BLOCK_BYTES = 1 << 20
OUT_ROWS = 256

_INV_FREQ = tuple(float(ROPE_THETA ** (-(2.0 * i) / ROT_DIMS)) for i in range(N_FREQ))

F32 = jnp.float32
BF16 = jnp.bfloat16


def _dot(a, b):
    return jnp.dot(a, b, preferred_element_type=F32)


def _rms_norm(x, g):
    ms = jnp.mean(x * x, axis=-1, keepdims=True)
    return x * lax.rsqrt(ms + NORM_EPS) * g


def _silu(z):
    return z * jax.nn.sigmoid(z)


def _rope_tables(pos_row, tm):
    sub = lax.broadcasted_iota(jnp.int32, (N_FREQ, tm), 0)
    inv_freq = jnp.full((N_FREQ, tm), _INV_FREQ[N_FREQ - 1], F32)
    for f in range(N_FREQ - 1):
        inv_freq = jnp.where(sub == f, _INV_FREQ[f], inv_freq)
    ang = pos_row.astype(F32) * inv_freq
    cos8 = jnp.cos(ang)
    sin8 = jnp.sin(ang)
    r = lax.broadcasted_iota(jnp.int32, (LANES, LANES), 0) & (HEAD_DIM - 1)
    c_blocks, s_blocks = [], []
    for blk in range(tm // LANES):
        c_rep = jnp.concatenate([cos8[:, blk * LANES:(blk + 1) * LANES]] * (LANES // N_FREQ), axis=0)
        s_rep = jnp.concatenate([sin8[:, blk * LANES:(blk + 1) * LANES]] * (LANES // N_FREQ), axis=0)
        c_pat = jnp.where(r < ROT_DIMS, c_rep, 1.0)
        s_pat = jnp.where(r < N_FREQ, -s_rep, jnp.where(r < ROT_DIMS, s_rep, 0.0))
        c_blocks.append(c_pat.T)
        s_blocks.append(s_pat.T)
    return jnp.concatenate(c_blocks, axis=0), jnp.concatenate(s_blocks, axis=0)


def _stage_shape(weight_shape):
    k_rows, width = weight_shape
    rows = min(k_rows, pl.next_power_of_2(STAGE_BLOCK_BYTES // (4 * width) + 1) // 2)
    return (min(STAGE_SLOTS, k_rows // rows), rows, width)


def _load_weight(src_hbm, dst, stage, sem):
    n_slots, rows, width = stage.shape
    n_blocks = src_hbm.shape[0] // rows

    def copy(k):
        return pltpu.make_async_copy(src_hbm.at[pl.ds(k * rows, rows), :], stage.at[k % n_slots], sem.at[k % n_slots])

    for k in range(min(n_slots - 1, n_blocks)):
        copy(k).start()
    for k in range(n_blocks):
        if k + n_slots - 1 < n_blocks:
            copy(k + n_slots - 1).start()
        copy(k).wait()
        dst[k * rows:(k + 1) * rows, 0:width] = stage[k % n_slots].astype(BF16)


def _first_step():
    return (pl.program_id(0) == 0) & (pl.program_id(1) == 0)


def _even_kernel(x_ref, g_ref, w_in_hbm, w_pool_hbm, ps_ref, conv_ref, w_out_hbm, o_ref,
                 ubuf, s2buf, s4buf, cbuf, w_in_ref, w_pool_ref, w_out_ref, stage_in, stage_pool, stage_out,
                 sem_in, sem_pool, sem_out, *, tm, tc):
    t = pl.program_id(1)
    n_slabs = D_MODEL // LANES
    slabs_per_group = POOL_GC // LANES
    n_s2, n_s4 = n_slabs - slabs_per_group, n_slabs - 2 * slabs_per_group
    ext = POOL_CARRY + tc

    @pl.when(_first_step())
    def _():
        _load_weight(w_in_hbm, w_in_ref, stage_in, sem_in)
        _load_weight(w_out_hbm, w_out_ref, stage_out, sem_out)
        _load_weight(w_pool_hbm, w_pool_ref, stage_pool, sem_pool)

    @pl.when(t == 0)
    def _():
        ubuf[:, 0:POOL_CARRY, :] = jnp.zeros((n_slabs, POOL_CARRY, LANES), F32)
        cbuf[:, 0:CONV_CARRY, :] = jnp.zeros((n_slabs, CONV_CARRY, LANES), F32)

    def chunk(c):
        r0 = c * tc
        x = x_ref[0, r0:r0 + tc, :]
        y = _rms_norm(x, g_ref[...]).astype(BF16)
        u = _dot(y, w_in_ref[:, 0:D_MODEL])
        gate_c = _dot(y, w_in_ref[:, 2 * D_MODEL:3 * D_MODEL])
        h_c = _dot(y, w_in_ref[:, 3 * D_MODEL:4 * D_MODEL])
        gate_b = _dot(y, w_in_ref[:, D_MODEL:2 * D_MODEL])
        z_b = _dot(y, w_in_ref[:, 5 * D_MODEL:6 * D_MODEL])
        z_a = _dot(y, w_in_ref[:, 4 * D_MODEL:5 * D_MODEL])

        pos = t * tm + r0 + lax.broadcasted_iota(jnp.int32, (tc, LANES), 0)
        a_parts = []
        for g, w in enumerate(POOL_WINDOWS):
            inv = 1.0 / jnp.minimum(pos + 1, w).astype(F32)
            slabs = []
            for j in range(g * slabs_per_group, (g + 1) * slabs_per_group):
                u_j = u[:, j * LANES:(j + 1) * LANES]
                ubuf[j, r0 + POOL_CARRY:r0 + ext, :] = u_j
                s = ubuf[j, r0 + 8:r0 + ext, :] + ubuf[j, r0 + 7:r0 + ext - 1, :]
                if w >= 4:
                    jj = c * n_s2 + j - slabs_per_group
                    s2buf[jj, :, :] = s
                    s = s[8:, :] + s2buf[jj, 6:ext - 10, :]
                if w >= 8:
                    jj = c * n_s4 + j - 2 * slabs_per_group
                    s4buf[jj, :, :] = s
                    s = s[8:, :] + s4buf[jj, 4:ext - 20, :]
                if w >= 16:
                    s = s[8:, :] + s[:-8, :]
                slabs.append(s[s.shape[0] - tc:, :] * inv - u_j)
            pooled = jnp.concatenate(slabs, axis=1).astype(BF16)
            a_parts.append(_dot(pooled, w_pool_ref[g * POOL_GC:(g + 1) * POOL_GC, :]))
        a = jnp.concatenate(a_parts, axis=1) * ps_ref[...]

        cu = gate_c * h_c
        cw = conv_ref[...]
        v_parts = []
        for j in range(n_slabs):
            lanes = slice(j * LANES, (j + 1) * LANES)
            cbuf[j, r0 + CONV_CARRY:r0 + CONV_CARRY + tc, :] = cu[:, lanes]
            v_parts.append(cw[2:3, lanes] * cu[:, lanes]
                           + cw[1:2, lanes] * cbuf[j, r0 + CONV_CARRY - 1:r0 + CONV_CARRY - 1 + tc, :]
                           + cw[0:1, lanes] * cbuf[j, r0 + CONV_CARRY - 2:r0 + CONV_CARRY - 2 + tc, :])
        bo = gate_b * jnp.concatenate(v_parts, axis=1)

        mix_b = (bo * _silu(z_b)).astype(BF16)
        mix_a = (a * _silu(z_a)).astype(BF16)
        o_ref[0, r0:r0 + tc, :] = x + (_dot(mix_b, w_out_ref[D_MODEL:, 0:D_MODEL])
                                       + _dot(mix_a, w_out_ref[0:D_MODEL, 0:D_MODEL]))

    for c in range(tm // tc):
        chunk(c)
    ubuf[:, 0:POOL_CARRY, :] = ubuf[:, tm:tm + POOL_CARRY, :]
    cbuf[:, 0:CONV_CARRY, :] = cbuf[:, tm:tm + CONV_CARRY, :]


def _const_spec(shape):
    nd = len(shape)
    return pl.BlockSpec(shape, lambda *_: (0,) * nd, pipeline_mode=pl.Buffered(1))


def _padded(shape):
    k_rows, width = shape
    return (k_rows, width + LANES) if (width // LANES) % 2 == 0 else (k_rows, width)


def _even_layer(x, g, w_in, w_pool, pool_scale, conv_w, w_out, *, tm, tc):
    b, s, d = x.shape
    tile = pl.BlockSpec((1, tm, d), lambda i, j: (i, j, 0))
    n_slabs, per_group = d // LANES, POOL_GC // LANES
    w_pool = w_pool.reshape(-1, w_pool.shape[-1])
    weight_bytes = 2 * (w_in.size + w_pool.size + w_out.size)
    stages = [_stage_shape(w.shape) for w in (w_in, w_pool, w_out)]
    stage_bytes = sum(4 * math.prod(st) for st in stages)
    tile_bytes, chunk_bytes = 4 * tm * d, 4 * tc * d
    assert weight_bytes + stage_bytes + (4 + 2) * tile_bytes + (3 + 10) * chunk_bytes <= VMEM_BYTES
    vmem_limit = VMEM_BYTES
    hbm = pl.BlockSpec(memory_space=pl.ANY)
    return pl.pallas_call(
        functools.partial(_even_kernel, tm=tm, tc=tc),
        grid=(b, s // tm),
        in_specs=[tile, _const_spec((1, d)), hbm, hbm, _const_spec((1, d)), _const_spec(conv_w.shape), hbm],
        out_specs=tile,
        out_shape=jax.ShapeDtypeStruct(x.shape, x.dtype),
        scratch_shapes=[pltpu.VMEM((n_slabs, POOL_CARRY + tm, LANES), F32),
                        pltpu.VMEM((tm // tc * (n_slabs - per_group), POOL_CARRY + tc - 8, LANES), F32),
                        pltpu.VMEM((tm // tc * (n_slabs - 2 * per_group), POOL_CARRY + tc - 16, LANES), F32),
                        pltpu.VMEM((n_slabs, CONV_CARRY + tm, LANES), F32),
                        pltpu.VMEM(_padded(w_in.shape), BF16),
                        pltpu.VMEM(w_pool.shape, BF16),
                        pltpu.VMEM(_padded(w_out.shape), BF16),
                        *[pltpu.VMEM(st, F32) for st in stages],
                        *[pltpu.SemaphoreType.DMA((st[0],)) for st in stages]],
        compiler_params=pltpu.CompilerParams(
            dimension_semantics=("arbitrary", "arbitrary"), vmem_limit_bytes=vmem_limit),
        name="even_layer",
    )(x, g.reshape(1, d), w_in, w_pool, pool_scale.reshape(1, d), conv_w, w_out)


def _odd_kernel(sink_ref, h_ref, pos_ref, g_ref, w_in_hbm, b_in_ref, w_out_hbm, b_out_ref, gf_ref,
                o_ref, kx, vtx, w_in_ref, w_out_ref, stage_in, stage_out, sem_in, sem_out, *, tm, tc):
    t = pl.program_id(1)
    nq = tc // WINDOW
    qkv_w = D_MODEL + 2 * KV_WIDTH

    @pl.when(_first_step())
    def _():
        _load_weight(w_in_hbm, w_in_ref, stage_in, sem_in)
        _load_weight(w_out_hbm, w_out_ref, stage_out, sem_out)

    @pl.when(t == 0)
    def _():
        kx[:, 0:WINDOW, :] = jnp.zeros((2 * N_KV_HEADS, WINDOW, LANES), BF16)
        vtx[:, 0:WINDOW] = jnp.zeros((KV_WIDTH, WINDOW), BF16)

    lane = lax.broadcasted_iota(jnp.int32, (tc, LANES), 1)
    first_half = (lane & (HEAD_DIM - 1)) < N_FREQ
    head0 = lane < HEAD_DIM

    def rope(xc, cc, sc):
        swapped = jnp.where(first_half, pltpu.roll(xc, LANES - N_FREQ, 1), pltpu.roll(xc, N_FREQ, 1))
        return xc * cc + swapped * sc

    slot = lax.broadcasted_iota(jnp.int32, (WINDOW, LANES), 0)
    qidx = lax.broadcasted_iota(jnp.int32, (WINDOW, LANES), 1)
    diff = slot - qidx
    eye = jnp.where(diff == 0, 1.0, 0.0).astype(BF16)
    masked = jnp.full((WINDOW, LANES), -MASK_BIAS, F32)
    bias_prev = jnp.where(diff > 0, 0.0, masked)
    bias_cur = jnp.where(diff <= 0, 0.0, masked)
    bias_mid = jnp.concatenate([bias_prev, bias_cur], axis=0).astype(BF16)
    bias_first = jnp.concatenate([jnp.where(t == 0, masked, bias_prev), bias_cur], axis=0).astype(BF16)
    ones_rows = jnp.ones((2 * SUBLANES, 2 * WINDOW), BF16)

    def chunk(c):
        r0 = c * tc
        h = h_ref[0, r0:r0 + tc, :]
        y = _rms_norm(h, g_ref[...]).astype(BF16)
        kv_p = _dot(y, w_in_ref[:, D_MODEL:qkv_w]) + b_in_ref[:, D_MODEL:qkv_w]
        proj = _dot(y, w_in_ref[:, 0:D_MODEL]) + b_in_ref[:, 0:D_MODEL]
        k = kv_p[:, 0:KV_WIDTH]
        v = kv_p[:, KV_WIDTH:]

        c_tab, s_tab = _rope_tables(pos_ref[0, :, r0:r0 + tc], tc)
        cq = c_tab * Q_SCALE
        sq = s_tab * Q_SCALE
        qr = [rope(proj[:, j * LANES:(j + 1) * LANES], cq, sq).astype(BF16) for j in range(D_MODEL // LANES)]
        kr = rope(k, c_tab, s_tab)

        k0, k1 = WINDOW + r0, WINDOW + r0 + tc
        lo0 = jnp.where(head0, kr, 0.0)
        hi1 = jnp.where(head0, 0.0, kr)
        kx[0, k0:k1, :] = lo0.astype(BF16)
        kx[1, k0:k1, :] = pltpu.roll(lo0, HEAD_DIM, 1).astype(BF16)
        kx[2, k0:k1, :] = pltpu.roll(hi1, HEAD_DIM, 1).astype(BF16)
        kx[3, k0:k1, :] = hi1.astype(BF16)
        for blk in range(nq):
            vtx[:, k0 + blk * WINDOW:k0 + (blk + 1) * WINDOW] = v[blk * WINDOW:(blk + 1) * WINDOW, :].T.astype(BF16)

        out_chunks = [[None] * nq for _ in range(D_MODEL // LANES)]
        units = [(qb, kv) for qb in range(nq) for kv in range(N_KV_HEADS)]

        def scores(qb, kv):
            a0 = r0 + qb * WINDOW
            bias = bias_first if a0 == 0 else bias_mid
            w_ext = jnp.concatenate(
                [jnp.concatenate([kx[2 * kv + par, a0:a0 + 2 * WINDOW, :], bias], axis=1) for par in range(2)],
                axis=0)
            q_ext = jnp.concatenate(
                [jnp.concatenate([qr[kv * PAIRS_PER_KV + j][qb * WINDOW:(qb + 1) * WINDOW, :], eye], axis=1)
                 for j in range(PAIRS_PER_KV)], axis=0)
            return lax.dot_general(w_ext, q_ext, (((1,), (1,)), ((), ())), preferred_element_type=F32)

        def attend(qb, kv, st):
            a0 = r0 + qb * WINDOW
            blocks = []
            terms = []
            for j in range(PAIRS_PER_KV):
                for par in range(2):
                    sink = sink_ref[kv * GROUP + 2 * j + par] * LOG2E
                    blk_s = st[par * 2 * WINDOW:(par + 1) * 2 * WINDOW, j * WINDOW:(j + 1) * WINDOW]
                    m = jnp.maximum(jnp.max(blk_s, axis=0, keepdims=True), sink)
                    blocks.append(jnp.exp2(blk_s - m).astype(BF16))
                    terms.append(jnp.exp2(sink - m))
            p_t = jnp.concatenate(blocks, axis=1)
            v_win = vtx[kv * HEAD_DIM:(kv + 1) * HEAD_DIM, a0:a0 + 2 * WINDOW]
            o_t = _dot(jnp.concatenate([v_win, ones_rows], axis=0), p_t)
            den = o_t[HEAD_DIM:HEAD_DIM + SUBLANES, :] + jnp.concatenate(terms, axis=1)
            n_t = o_t[:HEAD_DIM, :] * jnp.concatenate([1.0 / den] * (HEAD_DIM // SUBLANES), axis=0)
            for j in range(PAIRS_PER_KV):
                pair = jnp.concatenate([n_t[:, 2 * j * WINDOW:(2 * j + 1) * WINDOW],
                                        n_t[:, (2 * j + 1) * WINDOW:(2 * j + 2) * WINDOW]], axis=0)
                out_chunks[kv * PAIRS_PER_KV + j][qb] = pair.T

        z_cols = max(2 * LANES, D_MODEL // len(units))
        n_zc = D_MODEL // z_cols
        z_rows = tc * n_zc // len(units)
        z_parts = [[None] * n_zc for _ in range(tc // z_rows)]

        def gate_piece(i):
            z0, c0 = (i // n_zc) * z_rows, qkv_w + (i % n_zc) * z_cols
            z_parts[i // n_zc][i % n_zc] = (_dot(y[z0:z0 + z_rows, :], w_in_ref[:, c0:c0 + z_cols])
                                            + b_in_ref[:, c0:c0 + z_cols])

        gate_piece(0)
        st_next = scores(*units[0])
        for i, unit in enumerate(units):
            st = st_next
            if i + 1 < len(units):
                gate_piece(i + 1)
                st_next = scores(*units[i + 1])
            attend(*unit, st)

        z = jnp.concatenate([jnp.concatenate(row, axis=1) for row in z_parts], axis=0)
        attn = jnp.concatenate([jnp.concatenate(col, axis=0) for col in out_chunks], axis=1)
        gated = (attn * _silu(z)).astype(BF16)
        for p0 in range(0, tc, OUT_ROWS):
            h2 = h[p0:p0 + OUT_ROWS, :] + _dot(gated[p0:p0 + OUT_ROWS, :], w_out_ref[:, 0:D_MODEL]) + b_out_ref[...]
            o_ref[0, r0 + p0:r0 + p0 + OUT_ROWS, :] = _rms_norm(h2, gf_ref[...])

    for c in range(tm // tc):
        chunk(c)
    kx[:, 0:WINDOW, :] = kx[:, tm:tm + WINDOW, :]
    vtx[:, 0:WINDOW] = vtx[:, tm:tm + WINDOW]


def _odd_layer(h, positions, g, w_in, b_in, sinks, w_out, b_out, g_final, *, tm, tc):
    b, s, d = h.shape
    tile = pl.BlockSpec((1, tm, d), lambda i, j: (i, j, 0))
    pos_spec = pl.BlockSpec((1, 1, tm), lambda i, j: (i, 0, j))
    weight_bytes = 2 * (w_in.size + w_out.size)
    stages = [_stage_shape(w.shape) for w in (w_in, w_out)]
    stage_bytes = sum(4 * math.prod(st) for st in stages)
    tile_bytes = 4 * tm * d
    vmem_limit = weight_bytes + stage_bytes + (4 + 12) * tile_bytes + (8 << 20)
    hbm = pl.BlockSpec(memory_space=pl.ANY)
    return pl.pallas_call(
        functools.partial(_odd_kernel, tm=tm, tc=tc),
        grid=(b, s // tm),
        in_specs=[pl.BlockSpec(memory_space=pltpu.SMEM), tile, pos_spec, _const_spec((1, d)),
                  hbm, _const_spec((1, ODD_IN)), hbm, _const_spec((1, d)), _const_spec((1, d))],
        out_specs=tile,
        out_shape=jax.ShapeDtypeStruct(h.shape, h.dtype),
        scratch_shapes=[pltpu.VMEM((2 * N_KV_HEADS, WINDOW + tm, LANES), BF16),
                        pltpu.VMEM((KV_WIDTH, WINDOW + tm), BF16),
                        pltpu.VMEM(w_in.shape, BF16),
                        pltpu.VMEM(_padded(w_out.shape), BF16),
                        *[pltpu.VMEM(st, F32) for st in stages],
                        *[pltpu.SemaphoreType.DMA((st[0],)) for st in stages]],
        compiler_params=pltpu.CompilerParams(
            dimension_semantics=("arbitrary", "arbitrary"), vmem_limit_bytes=vmem_limit),
        name="odd_layer",
    )(sinks, h, positions.reshape(b, 1, s), g.reshape(1, d), w_in, b_in.reshape(1, ODD_IN), w_out,
      b_out.reshape(1, d), g_final.reshape(1, d))


EVEN_TM = 1024
EVEN_TC = 256
ODD_TM = 1024
ODD_TC = 256


def kernel(x, positions, norm_g, w_in_even, w_pool, pool_scale, conv_w, w_out_even, w_in_odd, b_in_odd,
           attn_sinks, w_out_odd, b_out_odd, final_norm_g):
    assert norm_g.shape[0] == 2 and w_in_even.shape[0] == 1 and w_in_odd.shape[0] == 1
    h = _even_layer(x, norm_g[0], w_in_even[0], w_pool[0], pool_scale[0], conv_w[0], w_out_even[0],
                    tm=EVEN_TM, tc=EVEN_TC)
    return _odd_layer(h, positions, norm_g[1], w_in_odd[0], b_in_odd[0], attn_sinks[0], w_out_odd[0], b_out_odd[0],
                      final_norm_g, tm=ODD_TM, tc=ODD_TC)
```

```python
import functools
import math

import jax
import jax.numpy as jnp
from jax import lax
from jax.experimental import pallas as pl
from jax.experimental.pallas import tpu as pltpu

D_MODEL = 1024
NORM_EPS = 1e-5
POOL_WINDOWS = (2, 4, 8, 16)
POOL_GC = D_MODEL // len(POOL_WINDOWS)
CONV_WIDTH = D_MODEL
EVEN_IN = 6 * D_MODEL
HEAD_DIM = 64
N_HEADS = 16
N_KV_HEADS = 2
GROUP = N_HEADS // N_KV_HEADS
KV_WIDTH = N_KV_HEADS * HEAD_DIM
ODD_IN = 2 * D_MODEL + 2 * KV_WIDTH
WINDOW = 128
ROPE_THETA = 500000.0
ROT_DIMS = HEAD_DIM // 4
N_FREQ = ROT_DIMS // 2

VMEM_BYTES = 64 << 20
LANES = 128
SUBLANES = 8
POOL_CARRY = 32
CONV_CARRY = SUBLANES
PAIRS_PER_KV = GROUP // 2
LOG2E = math.log2(math.e)
Q_SCALE = HEAD_DIM ** -0.5 * LOG2E
MASK_BIAS = 2.0 ** 100
STAGE_SLOTS = 4
STAGE_BLOCK_BYTES = 1 << 20
OUT_ROWS = 256

_INV_FREQ = tuple(float(ROPE_THETA ** (-(2.0 * i) / ROT_DIMS)) for i in range(N_FREQ))

F32 = jnp.float32
BF16 = jnp.bfloat16


def _dot(a, b):
    return jnp.dot(a, b, preferred_element_type=F32)


def _rms_norm(x, g):
    ms = jnp.mean(x * x, axis=-1, keepdims=True)
    return x * lax.rsqrt(ms + NORM_EPS) * g


def _silu(z):
    return z * jax.nn.sigmoid(z)


def _rope_angles(pos_row, tm):
    sub = lax.broadcasted_iota(jnp.int32, (N_FREQ, tm), 0)
    inv_freq = jnp.full((N_FREQ, tm), _INV_FREQ[N_FREQ - 1], F32)
    for f in range(N_FREQ - 1):
        inv_freq = jnp.where(sub == f, _INV_FREQ[f], inv_freq)
    ang = pos_row.astype(F32) * inv_freq
    return jnp.cos(ang), jnp.sin(ang)


def _rope_tables(cos8, sin8, tm):
    r = lax.broadcasted_iota(jnp.int32, (LANES, LANES), 0) & (HEAD_DIM - 1)
    c_blocks, s_blocks = [], []
    for blk in range(tm // LANES):
        c_rep = jnp.concatenate([cos8[:, blk * LANES:(blk + 1) * LANES]] * (LANES // N_FREQ), axis=0)
        s_rep = jnp.concatenate([sin8[:, blk * LANES:(blk + 1) * LANES]] * (LANES // N_FREQ), axis=0)
        c_pat = jnp.where(r < ROT_DIMS, c_rep, 1.0)
        s_pat = jnp.where(r < N_FREQ, -s_rep, jnp.where(r < ROT_DIMS, s_rep, 0.0))
        c_blocks.append(c_pat.T)
        s_blocks.append(s_pat.T)
    return jnp.concatenate(c_blocks, axis=0), jnp.concatenate(s_blocks, axis=0)


def _stage_shape(weight_shape):
    k_rows, width = weight_shape
    rows = min(k_rows, pl.next_power_of_2(STAGE_BLOCK_BYTES // (4 * width) + 1) // 2)
    return (min(STAGE_SLOTS, k_rows // rows), rows, width)


def _load_weight(src_hbm, dst, stage, sem, col0=0):
    n_slots, rows, width = stage.shape
    n_blocks = src_hbm.shape[0] // rows

    def copy(k):
        return pltpu.make_async_copy(src_hbm.at[pl.ds(k * rows, rows), pl.ds(col0, width)], stage.at[k % n_slots],
                                     sem.at[k % n_slots])

    for k in range(min(n_slots - 1, n_blocks)):
        copy(k).start()
    for k in range(n_blocks):
        if k + n_slots - 1 < n_blocks:
            copy(k + n_slots - 1).start()
        copy(k).wait()
        dst[k * rows:(k + 1) * rows, 0:width] = stage[k % n_slots].astype(BF16)


def _first_step():
    return (pl.program_id(0) == 0) & (pl.program_id(1) == 0)


def _even_kernel(x_ref, g_ref, w_in_hbm, w_pool_hbm, ps_ref, conv_ref, w_out_hbm, o_ref,
                 ubuf, s2buf, s4buf, cbuf, w_in_ref, w_pool_ref, w_out_ref, stage_in, stage_pool, stage_out,
                 sem_in, sem_pool, sem_out, *, tm, tc):
    t = pl.program_id(1)
    n_slabs = D_MODEL // LANES
    slabs_per_group = POOL_GC // LANES
    n_s2, n_s4 = n_slabs - slabs_per_group, n_slabs - 2 * slabs_per_group
    ext = POOL_CARRY + tc

    @pl.when(_first_step())
    def _():
        _load_weight(w_in_hbm, w_in_ref, stage_in, sem_in)
        _load_weight(w_out_hbm, w_out_ref, stage_out, sem_out)
        _load_weight(w_pool_hbm, w_pool_ref, stage_pool, sem_pool)

    @pl.when(t == 0)
    def _():
        ubuf[:, 0:POOL_CARRY, :] = jnp.zeros((n_slabs, POOL_CARRY, LANES), F32)
        cbuf[:, 0:CONV_CARRY, :] = jnp.zeros((n_slabs, CONV_CARRY, LANES), F32)

    def chunk(c):
        r0 = c * tc
        x = x_ref[0, r0:r0 + tc, :]
        y = _rms_norm(x, g_ref[...]).astype(BF16)
        u = _dot(y, w_in_ref[:, 0:D_MODEL])
        gate_c = _dot(y, w_in_ref[:, 2 * D_MODEL:3 * D_MODEL])
        h_c = _dot(y, w_in_ref[:, 3 * D_MODEL:4 * D_MODEL])
        gate_b = _dot(y, w_in_ref[:, D_MODEL:2 * D_MODEL])
        z_b = _dot(y, w_in_ref[:, 5 * D_MODEL:6 * D_MODEL])
        z_a = _dot(y, w_in_ref[:, 4 * D_MODEL:5 * D_MODEL])

        pos = t * tm + r0 + lax.broadcasted_iota(jnp.int32, (tc, LANES), 0)
        a_parts = []
        for g, w in enumerate(POOL_WINDOWS):
            inv = 1.0 / jnp.minimum(pos + 1, w).astype(F32)
            slabs = []
            for j in range(g * slabs_per_group, (g + 1) * slabs_per_group):
                u_j = u[:, j * LANES:(j + 1) * LANES]
                ubuf[j, r0 + POOL_CARRY:r0 + ext, :] = u_j
                s = ubuf[j, r0 + 8:r0 + ext, :] + ubuf[j, r0 + 7:r0 + ext - 1, :]
                if w >= 4:
                    jj = c * n_s2 + j - slabs_per_group
                    s2buf[jj, :, :] = s
                    s = s[8:, :] + s2buf[jj, 6:ext - 10, :]
                if w >= 8:
                    jj = c * n_s4 + j - 2 * slabs_per_group
                    s4buf[jj, :, :] = s
                    s = s[8:, :] + s4buf[jj, 4:ext - 20, :]
                if w >= 16:
                    s = s[8:, :] + s[:-8, :]
                slabs.append(s[s.shape[0] - tc:, :] * inv - u_j)
            pooled = jnp.concatenate(slabs, axis=1).astype(BF16)
            a_parts.append(_dot(pooled, w_pool_ref[g * POOL_GC:(g + 1) * POOL_GC, :]))
        a = jnp.concatenate(a_parts, axis=1) * ps_ref[...]

        cu = gate_c * h_c
        cw = conv_ref[...]
        v_parts = []
        for j in range(n_slabs):
            lanes = slice(j * LANES, (j + 1) * LANES)
            cbuf[j, r0 + CONV_CARRY:r0 + CONV_CARRY + tc, :] = cu[:, lanes]
            v_parts.append(cw[2:3, lanes] * cu[:, lanes]
                           + cw[1:2, lanes] * cbuf[j, r0 + CONV_CARRY - 1:r0 + CONV_CARRY - 1 + tc, :]
                           + cw[0:1, lanes] * cbuf[j, r0 + CONV_CARRY - 2:r0 + CONV_CARRY - 2 + tc, :])
        bo = gate_b * jnp.concatenate(v_parts, axis=1)

        mix_b = (bo * _silu(z_b)).astype(BF16)
        mix_a = (a * _silu(z_a)).astype(BF16)
        o_ref[0, r0:r0 + tc, :] = x + (_dot(mix_b, w_out_ref[D_MODEL:, 0:D_MODEL])
                                       + _dot(mix_a, w_out_ref[0:D_MODEL, 0:D_MODEL]))

    for c in range(tm // tc):
        chunk(c)
    ubuf[:, 0:POOL_CARRY, :] = ubuf[:, tm:tm + POOL_CARRY, :]
    cbuf[:, 0:CONV_CARRY, :] = cbuf[:, tm:tm + CONV_CARRY, :]


def _const_spec(shape):
    nd = len(shape)
    return pl.BlockSpec(shape, lambda *_: (0,) * nd, pipeline_mode=pl.Buffered(1))


def _padded(shape):
    k_rows, width = shape
    return (k_rows, width + LANES) if (width // LANES) % 2 == 0 else (k_rows, width)


def _even_layer(x, g, w_in, w_pool, pool_scale, conv_w, w_out, *, tm, tc):
    b, s, d = x.shape
    tile = pl.BlockSpec((1, tm, d), lambda i, j: (i, j, 0))
    n_slabs, per_group = d // LANES, POOL_GC // LANES
    w_pool = w_pool.reshape(-1, w_pool.shape[-1])
    weight_bytes = 2 * (w_in.size + w_pool.size + w_out.size)
    stages = [_stage_shape(w.shape) for w in (w_in, w_pool, w_out)]
    stage_bytes = sum(4 * math.prod(st) for st in stages)
    tile_bytes, chunk_bytes = 4 * tm * d, 4 * tc * d
    assert weight_bytes + stage_bytes + (4 + 2) * tile_bytes + (3 + 10) * chunk_bytes <= VMEM_BYTES
    vmem_limit = VMEM_BYTES
    hbm = pl.BlockSpec(memory_space=pl.ANY)
    return pl.pallas_call(
        functools.partial(_even_kernel, tm=tm, tc=tc),
        grid=(b, s // tm),
        in_specs=[tile, _const_spec((1, d)), hbm, hbm, _const_spec((1, d)), _const_spec(conv_w.shape), hbm],
        out_specs=tile,
        out_shape=jax.ShapeDtypeStruct(x.shape, x.dtype),
        scratch_shapes=[pltpu.VMEM((n_slabs, POOL_CARRY + tm, LANES), F32),
                        pltpu.VMEM((tm // tc * (n_slabs - per_group), POOL_CARRY + tc - 8, LANES), F32),
                        pltpu.VMEM((tm // tc * (n_slabs - 2 * per_group), POOL_CARRY + tc - 16, LANES), F32),
                        pltpu.VMEM((n_slabs, CONV_CARRY + tm, LANES), F32),
                        pltpu.VMEM(_padded(w_in.shape), BF16),
                        pltpu.VMEM(w_pool.shape, BF16),
                        pltpu.VMEM(_padded(w_out.shape), BF16),
                        *[pltpu.VMEM(st, F32) for st in stages],
                        *[pltpu.SemaphoreType.DMA((st[0],)) for st in stages]],
        compiler_params=pltpu.CompilerParams(
            dimension_semantics=("arbitrary", "arbitrary"), vmem_limit_bytes=vmem_limit),
        name="even_layer",
    )(x, g.reshape(1, d), w_in, w_pool, pool_scale.reshape(1, d), conv_w, w_out)


def _odd_kernel(sink_ref, h_ref, pos_ref, g_ref, w_qt_hbm, bq_ref, w_in_hbm, b_in_ref, w_out_hbm, b_out_ref, gf_ref,
                o_ref, kx, vtx, w_qt_ref, w_in_ref, w_out_ref, stage_qt, stage_in, stage_out, sem_qt, sem_in, sem_out,
                *, tm, tc):
    t = pl.program_id(1)
    nq = tc // WINDOW
    qkv_w = D_MODEL + 2 * KV_WIDTH

    @pl.when(_first_step())
    def _():
        _load_weight(w_qt_hbm, w_qt_ref, stage_qt, sem_qt)
        _load_weight(w_in_hbm, w_in_ref, stage_in, sem_in, col0=D_MODEL)
        _load_weight(w_out_hbm, w_out_ref, stage_out, sem_out)

    @pl.when(t == 0)
    def _():
        kx[:, 0:WINDOW, :] = jnp.zeros((2 * N_KV_HEADS, WINDOW, LANES), BF16)
        vtx[:, 0:WINDOW] = jnp.zeros((KV_WIDTH, WINDOW), BF16)

    lane = lax.broadcasted_iota(jnp.int32, (tc, LANES), 1)
    first_half = (lane & (HEAD_DIM - 1)) < N_FREQ
    head0 = lane < HEAD_DIM

    def rope(xc, cc, sc):
        swapped = jnp.where(first_half, pltpu.roll(xc, LANES - N_FREQ, 1), pltpu.roll(xc, N_FREQ, 1))
        return xc * cc + swapped * sc

    slot = lax.broadcasted_iota(jnp.int32, (WINDOW, LANES), 0)
    qidx = lax.broadcasted_iota(jnp.int32, (WINDOW, LANES), 1)
    diff = slot - qidx
    eye = jnp.where(diff == 0, 1.0, 0.0).astype(BF16)
    masked = jnp.full((WINDOW, LANES), -MASK_BIAS, F32)
    bias_prev = jnp.where(diff > 0, 0.0, masked)
    bias_cur = jnp.where(diff <= 0, 0.0, masked)
    bias_mid = jnp.concatenate([bias_prev, bias_cur], axis=0).astype(BF16)
    bias_first = jnp.concatenate([jnp.where(t == 0, masked, bias_prev), bias_cur], axis=0).astype(BF16)
    ones_rows = jnp.ones((2 * SUBLANES, 2 * WINDOW), BF16)

    def chunk(c):
        r0 = c * tc
        h = h_ref[0, r0:r0 + tc, :]
        y = _rms_norm(h, g_ref[...]).astype(BF16)
        kv_p = _dot(y, w_in_ref[:, 0:2 * KV_WIDTH]) + b_in_ref[:, D_MODEL:qkv_w]
        k = kv_p[:, 0:KV_WIDTH]
        v = kv_p[:, KV_WIDTH:]
        q_t = lax.dot_general(w_qt_ref[:, 0:D_MODEL], y, (((1,), (1,)), ((), ())), preferred_element_type=F32)
        q_t = q_t + jnp.concatenate([bq_ref[...]] * (tc // LANES), axis=1)

        cos8, sin8 = _rope_angles(pos_ref[0, :, r0:r0 + tc], tc)
        cq8, sq8 = cos8 * Q_SCALE, sin8 * Q_SCALE
        q_rows = []
        for hd in range(N_HEADS):
            x1 = q_t[hd * HEAD_DIM:hd * HEAD_DIM + N_FREQ, :]
            x2 = q_t[hd * HEAD_DIM + N_FREQ:hd * HEAD_DIM + ROT_DIMS, :]
            q_rows += [x1 * cq8 - x2 * sq8, x2 * cq8 + x1 * sq8,
                       q_t[hd * HEAD_DIM + ROT_DIMS:(hd + 1) * HEAD_DIM, :] * Q_SCALE]
        qr_t = jnp.concatenate(q_rows, axis=0).astype(BF16)
        c_tab, s_tab = _rope_tables(cos8, sin8, tc)
        kr = rope(k, c_tab, s_tab)

        k0, k1 = WINDOW + r0, WINDOW + r0 + tc
        lo0 = jnp.where(head0, kr, 0.0)
        hi1 = jnp.where(head0, 0.0, kr)
        kx[0, k0:k1, :] = lo0.astype(BF16)
        kx[1, k0:k1, :] = pltpu.roll(lo0, HEAD_DIM, 1).astype(BF16)
        kx[2, k0:k1, :] = pltpu.roll(hi1, HEAD_DIM, 1).astype(BF16)
        kx[3, k0:k1, :] = hi1.astype(BF16)
        for blk in range(nq):
            vtx[:, k0 + blk * WINDOW:k0 + (blk + 1) * WINDOW] = v[blk * WINDOW:(blk + 1) * WINDOW, :].T.astype(BF16)

        out_chunks = [[None] * nq for _ in range(D_MODEL // LANES)]
        units = [(qb, kv) for qb in range(nq) for kv in range(N_KV_HEADS)]

        def scores(qb, kv):
            a0 = r0 + qb * WINDOW
            bias = bias_first if a0 == 0 else bias_mid
            w_ext = jnp.concatenate(
                [jnp.concatenate([kx[2 * kv + par, a0:a0 + 2 * WINDOW, :], bias], axis=1) for par in range(2)],
                axis=0)
            q_ext = jnp.concatenate(
                [jnp.concatenate([qr_t[(kv * PAIRS_PER_KV + j) * LANES:(kv * PAIRS_PER_KV + j + 1) * LANES,
                                       qb * WINDOW:(qb + 1) * WINDOW], eye], axis=0)
                 for j in range(PAIRS_PER_KV)], axis=1)
            return _dot(w_ext, q_ext)

        def attend(qb, kv, st):
            a0 = r0 + qb * WINDOW
            blocks = []
            terms = []
            for j in range(PAIRS_PER_KV):
                for par in range(2):
                    sink = sink_ref[kv * GROUP + 2 * j + par] * LOG2E
                    blk_s = st[par * 2 * WINDOW:(par + 1) * 2 * WINDOW, j * WINDOW:(j + 1) * WINDOW]
                    m = jnp.maximum(jnp.max(blk_s, axis=0, keepdims=True), sink)
                    blocks.append(jnp.exp2(blk_s - m).astype(BF16))
                    terms.append(jnp.exp2(sink - m))
            p_t = jnp.concatenate(blocks, axis=1)
            v_win = vtx[kv * HEAD_DIM:(kv + 1) * HEAD_DIM, a0:a0 + 2 * WINDOW]
            o_t = _dot(jnp.concatenate([v_win, ones_rows], axis=0), p_t)
            den = o_t[HEAD_DIM:HEAD_DIM + SUBLANES, :] + jnp.concatenate(terms, axis=1)
            n_t = o_t[:HEAD_DIM, :] * jnp.concatenate([1.0 / den] * (HEAD_DIM // SUBLANES), axis=0)
            for j in range(PAIRS_PER_KV):
                pair = jnp.concatenate([n_t[:, 2 * j * WINDOW:(2 * j + 1) * WINDOW],
                                        n_t[:, (2 * j + 1) * WINDOW:(2 * j + 2) * WINDOW]], axis=0)
                out_chunks[kv * PAIRS_PER_KV + j][qb] = pair.T

        z_cols = max(2 * LANES, D_MODEL // len(units))
        n_zc = D_MODEL // z_cols
        z_rows = tc * n_zc // len(units)
        z_parts = [[None] * n_zc for _ in range(tc // z_rows)]

        def gate_piece(i):
            z0, c0 = (i // n_zc) * z_rows, 2 * KV_WIDTH + (i % n_zc) * z_cols
            z_parts[i // n_zc][i % n_zc] = (_dot(y[z0:z0 + z_rows, :], w_in_ref[:, c0:c0 + z_cols])
                                            + b_in_ref[:, D_MODEL + c0:D_MODEL + c0 + z_cols])

        gate_piece(0)
        st_next = scores(*units[0])
        for i, unit in enumerate(units):
            st = st_next
            if i + 1 < len(units):
                gate_piece(i + 1)
                st_next = scores(*units[i + 1])
            attend(*unit, st)

        z = jnp.concatenate([jnp.concatenate(row, axis=1) for row in z_parts], axis=0)
        attn = jnp.concatenate([jnp.concatenate(col, axis=0) for col in out_chunks], axis=1)
        gated = (attn * _silu(z)).astype(BF16)
        for p0 in range(0, tc, OUT_ROWS):
            h2 = h[p0:p0 + OUT_ROWS, :] + _dot(gated[p0:p0 + OUT_ROWS, :], w_out_ref[:, 0:D_MODEL]) + b_out_ref[...]
            o_ref[0, r0 + p0:r0 + p0 + OUT_ROWS, :] = _rms_norm(h2, gf_ref[...])

    for c in range(tm // tc):
        chunk(c)
    kx[:, 0:WINDOW, :] = kx[:, tm:tm + WINDOW, :]
    vtx[:, 0:WINDOW] = vtx[:, tm:tm + WINDOW]


def _odd_layer(h, positions, g, w_in, b_in, sinks, w_out, b_out, g_final, *, tm, tc):
    b, s, d = h.shape
    tile = pl.BlockSpec((1, tm, d), lambda i, j: (i, j, 0))
    pos_spec = pl.BlockSpec((1, 1, tm), lambda i, j: (i, 0, j))
    w_qt = w_in[:, 0:d].T
    bq = jnp.broadcast_to(b_in[0:d, None], (d, LANES))
    kvz_shape = (w_in.shape[0], w_in.shape[1] - d)
    weight_bytes = 2 * (w_in.size + w_out.size)
    stages = [_stage_shape(shape) for shape in (w_qt.shape, kvz_shape, w_out.shape)]
    stage_bytes = sum(4 * math.prod(st) for st in stages)
    tile_bytes = 4 * tm * d
    vmem_limit = weight_bytes + stage_bytes + (4 + 12) * tile_bytes + (8 << 20)
    hbm = pl.BlockSpec(memory_space=pl.ANY)
    return pl.pallas_call(
        functools.partial(_odd_kernel, tm=tm, tc=tc),
        grid=(b, s // tm),
        in_specs=[pl.BlockSpec(memory_space=pltpu.SMEM), tile, pos_spec, _const_spec((1, d)),
                  hbm, _const_spec((d, LANES)), hbm, _const_spec((1, ODD_IN)), hbm, _const_spec((1, d)),
                  _const_spec((1, d))],
        out_specs=tile,
        out_shape=jax.ShapeDtypeStruct(h.shape, h.dtype),
        scratch_shapes=[pltpu.VMEM((2 * N_KV_HEADS, WINDOW + tm, LANES), BF16),
                        pltpu.VMEM((KV_WIDTH, WINDOW + tm), BF16),
                        pltpu.VMEM(_padded(w_qt.shape), BF16),
                        pltpu.VMEM(kvz_shape, BF16),
                        pltpu.VMEM(_padded(w_out.shape), BF16),
                        *[pltpu.VMEM(st, F32) for st in stages],
                        *[pltpu.SemaphoreType.DMA((st[0],)) for st in stages]],
        compiler_params=pltpu.CompilerParams(
            dimension_semantics=("arbitrary", "arbitrary"), vmem_limit_bytes=vmem_limit),
        name="odd_layer",
    )(sinks, h, positions.reshape(b, 1, s), g.reshape(1, d), w_qt, bq, w_in, b_in.reshape(1, ODD_IN), w_out,
      b_out.reshape(1, d), g_final.reshape(1, d))


EVEN_TM = 1024
EVEN_TC = 256
ODD_TM = 1024
ODD_TC = 512


def kernel(x, positions, norm_g, w_in_even, w_pool, pool_scale, conv_w, w_out_even, w_in_odd, b_in_odd,
           attn_sinks, w_out_odd, b_out_odd, final_norm_g):
    assert norm_g.shape[0] == 2 and w_in_even.shape[0] == 1 and w_in_odd.shape[0] == 1
    h = _even_layer(x, norm_g[0], w_in_even[0], w_pool[0], pool_scale[0], conv_w[0], w_out_even[0],
                    tm=EVEN_TM, tc=EVEN_TC)
    return _odd_layer(h, positions, norm_g[1], w_in_odd[0], b_in_odd[0], attn_sinks[0], w_out_odd[0], b_out_odd[0],
                      final_norm_g, tm=ODD_TM, tc=ODD_TC)
```

```python
import functools
import math

import jax
import jax.numpy as jnp
from jax import lax
from jax.experimental import pallas as pl
from jax.experimental.pallas import tpu as pltpu

D_MODEL = 1024
NORM_EPS = 1e-5
POOL_WINDOWS = (2, 4, 8, 16)
POOL_GC = D_MODEL // len(POOL_WINDOWS)
CONV_WIDTH = D_MODEL
EVEN_IN = 6 * D_MODEL
HEAD_DIM = 64
N_HEADS = 16
N_KV_HEADS = 2
GROUP = N_HEADS // N_KV_HEADS
KV_WIDTH = N_KV_HEADS * HEAD_DIM
ODD_IN = 2 * D_MODEL + 2 * KV_WIDTH
WINDOW = 128
ROPE_THETA = 500000.0
ROT_DIMS = HEAD_DIM // 4
N_FREQ = ROT_DIMS // 2

VMEM_BYTES = 64 << 20
LANES = 128
SUBLANES = 8
POOL_CARRY = 32
CONV_CARRY = SUBLANES
PAIRS_PER_KV = GROUP // 2
LOG2E = math.log2(math.e)
Q_SCALE = HEAD_DIM ** -0.5 * LOG2E
MASK_BIAS = 2.0 ** 100
STAGE_SLOTS = 4
STAGE_BLOCK_BYTES = 1 << 20
OUT_ROWS = 256

_INV_FREQ = tuple(float(ROPE_THETA ** (-(2.0 * i) / ROT_DIMS)) for i in range(N_FREQ))

F32 = jnp.float32
BF16 = jnp.bfloat16


def _dot(a, b):
    return jnp.dot(a, b, preferred_element_type=F32)


def _rms_norm(x, g):
    ms = jnp.mean(x * x, axis=-1, keepdims=True)
    return x * lax.rsqrt(ms + NORM_EPS) * g


def _silu(z):
    return z * jax.nn.sigmoid(z)


def _rope_angles(pos_row, tm):
    sub = lax.broadcasted_iota(jnp.int32, (N_FREQ, tm), 0)
    inv_freq = jnp.full((N_FREQ, tm), _INV_FREQ[N_FREQ - 1], F32)
    for f in range(N_FREQ - 1):
        inv_freq = jnp.where(sub == f, _INV_FREQ[f], inv_freq)
    ang = pos_row.astype(F32) * inv_freq
    return jnp.cos(ang), jnp.sin(ang)


def _stage_shape(weight_shape):
    k_rows, width = weight_shape
    rows = min(k_rows, pl.next_power_of_2(STAGE_BLOCK_BYTES // (4 * width) + 1) // 2)
    return (min(STAGE_SLOTS, k_rows // rows), rows, width)


def _load_weight(src_hbm, dst, stage, sem, col0=0):
    n_slots, rows, width = stage.shape
    n_blocks = src_hbm.shape[0] // rows

    def copy(k):
        return pltpu.make_async_copy(src_hbm.at[pl.ds(k * rows, rows), pl.ds(col0, width)], stage.at[k % n_slots],
                                     sem.at[k % n_slots])

    for k in range(min(n_slots - 1, n_blocks)):
        copy(k).start()
    for k in range(n_blocks):
        if k + n_slots - 1 < n_blocks:
            copy(k + n_slots - 1).start()
        copy(k).wait()
        dst[k * rows:(k + 1) * rows, 0:width] = stage[k % n_slots].astype(BF16)


def _first_step():
    return (pl.program_id(0) == 0) & (pl.program_id(1) == 0)


def _even_kernel(x_ref, g_ref, w_in_hbm, w_pool_hbm, ps_ref, conv_ref, w_out_hbm, o_ref,
                 ubuf, s2buf, s4buf, cbuf, w_in_ref, w_pool_ref, w_out_ref, stage_in, stage_pool, stage_out,
                 sem_in, sem_pool, sem_out, *, tm, tc):
    t = pl.program_id(1)
    n_slabs = D_MODEL // LANES
    slabs_per_group = POOL_GC // LANES
    n_s2, n_s4 = n_slabs - slabs_per_group, n_slabs - 2 * slabs_per_group
    ext = POOL_CARRY + tc

    @pl.when(_first_step())
    def _():
        _load_weight(w_in_hbm, w_in_ref, stage_in, sem_in)
        _load_weight(w_out_hbm, w_out_ref, stage_out, sem_out)
        _load_weight(w_pool_hbm, w_pool_ref, stage_pool, sem_pool)

    @pl.when(t == 0)
    def _():
        ubuf[:, 0:POOL_CARRY, :] = jnp.zeros((n_slabs, POOL_CARRY, LANES), F32)
        cbuf[:, 0:CONV_CARRY, :] = jnp.zeros((n_slabs, CONV_CARRY, LANES), F32)

    def chunk(c):
        r0 = c * tc
        x = x_ref[0, r0:r0 + tc, :]
        y = _rms_norm(x, g_ref[...]).astype(BF16)
        u = _dot(y, w_in_ref[:, 0:D_MODEL])
        gate_c = _dot(y, w_in_ref[:, 2 * D_MODEL:3 * D_MODEL])
        h_c = _dot(y, w_in_ref[:, 3 * D_MODEL:4 * D_MODEL])
        gate_b = _dot(y, w_in_ref[:, D_MODEL:2 * D_MODEL])
        z_b = _dot(y, w_in_ref[:, 5 * D_MODEL:6 * D_MODEL])
        z_a = _dot(y, w_in_ref[:, 4 * D_MODEL:5 * D_MODEL])

        pos = t * tm + r0 + lax.broadcasted_iota(jnp.int32, (tc, LANES), 0)
        a_parts = []
        for g, w in enumerate(POOL_WINDOWS):
            inv = 1.0 / jnp.minimum(pos + 1, w).astype(F32)
            slabs = []
            for j in range(g * slabs_per_group, (g + 1) * slabs_per_group):
                u_j = u[:, j * LANES:(j + 1) * LANES]
                ubuf[j, r0 + POOL_CARRY:r0 + ext, :] = u_j
                s = ubuf[j, r0 + 8:r0 + ext, :] + ubuf[j, r0 + 7:r0 + ext - 1, :]
                if w >= 4:
                    jj = c * n_s2 + j - slabs_per_group
                    s2buf[jj, :, :] = s
                    s = s[8:, :] + s2buf[jj, 6:ext - 10, :]
                if w >= 8:
                    jj = c * n_s4 + j - 2 * slabs_per_group
                    s4buf[jj, :, :] = s
                    s = s[8:, :] + s4buf[jj, 4:ext - 20, :]
                if w >= 16:
                    s = s[8:, :] + s[:-8, :]
                slabs.append(s[s.shape[0] - tc:, :] * inv - u_j)
            pooled = jnp.concatenate(slabs, axis=1).astype(BF16)
            a_parts.append(_dot(pooled, w_pool_ref[g * POOL_GC:(g + 1) * POOL_GC, :]))
        a = jnp.concatenate(a_parts, axis=1) * ps_ref[...]

        cu = gate_c * h_c
        cw = conv_ref[...]
        v_parts = []
        for j in range(n_slabs):
            lanes = slice(j * LANES, (j + 1) * LANES)
            cbuf[j, r0 + CONV_CARRY:r0 + CONV_CARRY + tc, :] = cu[:, lanes]
            v_parts.append(cw[2:3, lanes] * cu[:, lanes]
                           + cw[1:2, lanes] * cbuf[j, r0 + CONV_CARRY - 1:r0 + CONV_CARRY - 1 + tc, :]
                           + cw[0:1, lanes] * cbuf[j, r0 + CONV_CARRY - 2:r0 + CONV_CARRY - 2 + tc, :])
        bo = gate_b * jnp.concatenate(v_parts, axis=1)

        mix_b = (bo * _silu(z_b)).astype(BF16)
        mix_a = (a * _silu(z_a)).astype(BF16)
        o_ref[0, r0:r0 + tc, :] = x + (_dot(mix_b, w_out_ref[D_MODEL:, 0:D_MODEL])
                                       + _dot(mix_a, w_out_ref[0:D_MODEL, 0:D_MODEL]))

    for c in range(tm // tc):
        chunk(c)
    ubuf[:, 0:POOL_CARRY, :] = ubuf[:, tm:tm + POOL_CARRY, :]
    cbuf[:, 0:CONV_CARRY, :] = cbuf[:, tm:tm + CONV_CARRY, :]


def _const_spec(shape):
    nd = len(shape)
    return pl.BlockSpec(shape, lambda *_: (0,) * nd, pipeline_mode=pl.Buffered(1))


def _padded(shape):
    k_rows, width = shape
    return (k_rows, width + LANES) if (width // LANES) % 2 == 0 else (k_rows, width)


def _even_layer(x, g, w_in, w_pool, pool_scale, conv_w, w_out, *, tm, tc):
    b, s, d = x.shape
    tile = pl.BlockSpec((1, tm, d), lambda i, j: (i, j, 0))
    n_slabs, per_group = d // LANES, POOL_GC // LANES
    w_pool = w_pool.reshape(-1, w_pool.shape[-1])
    weight_bytes = 2 * (w_in.size + w_pool.size + w_out.size)
    stages = [_stage_shape(w.shape) for w in (w_in, w_pool, w_out)]
    stage_bytes = sum(4 * math.prod(st) for st in stages)
    tile_bytes, chunk_bytes = 4 * tm * d, 4 * tc * d
    assert weight_bytes + stage_bytes + (4 + 2) * tile_bytes + (3 + 10) * chunk_bytes <= VMEM_BYTES
    vmem_limit = VMEM_BYTES
    hbm = pl.BlockSpec(memory_space=pl.ANY)
    return pl.pallas_call(
        functools.partial(_even_kernel, tm=tm, tc=tc),
        grid=(b, s // tm),
        in_specs=[tile, _const_spec((1, d)), hbm, hbm, _const_spec((1, d)), _const_spec(conv_w.shape), hbm],
        out_specs=tile,
        out_shape=jax.ShapeDtypeStruct(x.shape, x.dtype),
        scratch_shapes=[pltpu.VMEM((n_slabs, POOL_CARRY + tm, LANES), F32),
                        pltpu.VMEM((tm // tc * (n_slabs - per_group), POOL_CARRY + tc - 8, LANES), F32),
                        pltpu.VMEM((tm // tc * (n_slabs - 2 * per_group), POOL_CARRY + tc - 16, LANES), F32),
                        pltpu.VMEM((n_slabs, CONV_CARRY + tm, LANES), F32),
                        pltpu.VMEM(_padded(w_in.shape), BF16),
                        pltpu.VMEM(w_pool.shape, BF16),
                        pltpu.VMEM(_padded(w_out.shape), BF16),
                        *[pltpu.VMEM(st, F32) for st in stages],
                        *[pltpu.SemaphoreType.DMA((st[0],)) for st in stages]],
        compiler_params=pltpu.CompilerParams(
            dimension_semantics=("arbitrary", "arbitrary"), vmem_limit_bytes=vmem_limit),
        name="even_layer",
    )(x, g.reshape(1, d), w_in, w_pool, pool_scale.reshape(1, d), conv_w, w_out)


def _odd_kernel(sink_ref, h_ref, pos_ref, g_ref, w_qkv_hbm, b_qkv_ref, w_in_hbm, b_in_ref, w_out_hbm, b_out_ref, gf_ref,
                o_ref, kx, vtx, w_qkv_ref, w_z_ref, w_out_ref, stage_qkv, stage_z, stage_out, sem_qkv, sem_z, sem_out,
                *, tm, tc):
    t = pl.program_id(1)
    nq = tc // WINDOW
    qkv_w = D_MODEL + 2 * KV_WIDTH

    @pl.when(_first_step())
    def _():
        _load_weight(w_qkv_hbm, w_qkv_ref, stage_qkv, sem_qkv)
        _load_weight(w_in_hbm, w_z_ref, stage_z, sem_z, col0=qkv_w)
        _load_weight(w_out_hbm, w_out_ref, stage_out, sem_out)

    @pl.when(t == 0)
    def _():
        kx[0:WINDOW, :] = jnp.zeros((WINDOW, LANES), BF16)
        vtx[:, 0:WINDOW] = jnp.zeros((KV_WIDTH, WINDOW), BF16)

    slot = lax.broadcasted_iota(jnp.int32, (WINDOW, LANES), 0)
    qidx = lax.broadcasted_iota(jnp.int32, (WINDOW, LANES), 1)
    diff = slot - qidx
    eye = jnp.where(diff == 0, 1.0, 0.0).astype(BF16)
    masked = jnp.full((WINDOW, LANES), -MASK_BIAS, F32)
    bias_prev = jnp.where(diff > 0, 0.0, masked)
    bias_cur = jnp.where(diff <= 0, 0.0, masked)
    bias_mid = jnp.concatenate([bias_prev, bias_cur], axis=0).astype(BF16)
    bias_first = jnp.concatenate([jnp.where(t == 0, masked, bias_prev), bias_cur], axis=0).astype(BF16)
    ones_rows = jnp.ones((2 * SUBLANES, 2 * WINDOW), BF16)
    no_head = jnp.zeros((HEAD_DIM, WINDOW), BF16)

    def rotary(x_t, cos8, sin8, scale):
        x1, x2 = x_t[0:N_FREQ, :], x_t[N_FREQ:ROT_DIMS, :]
        rest = x_t[ROT_DIMS:, :]
        return [x1 * cos8 - x2 * sin8, x2 * cos8 + x1 * sin8, rest if scale == 1.0 else rest * scale]

    def chunk(c):
        r0 = c * tc
        h = h_ref[0, r0:r0 + tc, :]
        y = _rms_norm(h, g_ref[...]).astype(BF16)
        qkv_t = lax.dot_general(w_qkv_ref[:, 0:D_MODEL], y, (((1,), (1,)), ((), ())), preferred_element_type=F32)
        qkv_t = qkv_t + jnp.concatenate([b_qkv_ref[...]] * (tc // LANES), axis=1)

        cos8, sin8 = _rope_angles(pos_ref[0, :, r0:r0 + tc], tc)
        cq8, sq8 = cos8 * Q_SCALE, sin8 * Q_SCALE
        q_rows = []
        for hd in range(N_HEADS):
            q_rows += rotary(qkv_t[hd * HEAD_DIM:(hd + 1) * HEAD_DIM, :], cq8, sq8, Q_SCALE)
        qr_t = jnp.concatenate(q_rows, axis=0).astype(BF16)
        k_rows = []
        for kv in range(N_KV_HEADS):
            k_rows += rotary(qkv_t[D_MODEL + kv * HEAD_DIM:D_MODEL + (kv + 1) * HEAD_DIM, :], cos8, sin8, 1.0)
        kr_t = jnp.concatenate(k_rows, axis=0)
        k0, k1 = WINDOW + r0, WINDOW + r0 + tc
        for blk in range(nq):
            kx[k0 + blk * WINDOW:k0 + (blk + 1) * WINDOW, :] = kr_t[:, blk * WINDOW:(blk + 1) * WINDOW].T.astype(BF16)
        vtx[:, k0:k1] = qkv_t[D_MODEL + KV_WIDTH:qkv_w, :].astype(BF16)

        out_chunks = [[None] * nq for _ in range(D_MODEL // LANES)]
        units = [(qb, kv) for qb in range(nq) for kv in range(N_KV_HEADS)]

        def scores(qb, kv):
            a0 = r0 + qb * WINDOW
            bias = bias_first if a0 == 0 else bias_mid
            w_ext = jnp.concatenate([kx[a0:a0 + 2 * WINDOW, :], bias], axis=1)
            cols = []
            for g in range(GROUP):
                hd = kv * GROUP + g
                q_h = qr_t[hd * HEAD_DIM:(hd + 1) * HEAD_DIM, qb * WINDOW:(qb + 1) * WINDOW]
                cols.append(jnp.concatenate([q_h, no_head, eye] if kv == 0 else [no_head, q_h, eye], axis=0))
            return _dot(w_ext, jnp.concatenate(cols, axis=1))

        def attend(qb, kv, st):
            a0 = r0 + qb * WINDOW
            blocks = []
            terms = []
            for g in range(GROUP):
                sink = sink_ref[kv * GROUP + g] * LOG2E
                blk_s = st[:, g * WINDOW:(g + 1) * WINDOW]
                m = jnp.maximum(jnp.max(blk_s, axis=0, keepdims=True), sink)
                blocks.append(jnp.exp2(blk_s - m).astype(BF16))
                terms.append(jnp.exp2(sink - m))
            p_t = jnp.concatenate(blocks, axis=1)
            v_win = vtx[kv * HEAD_DIM:(kv + 1) * HEAD_DIM, a0:a0 + 2 * WINDOW]
            o_t = _dot(jnp.concatenate([v_win, ones_rows], axis=0), p_t)
            den = o_t[HEAD_DIM:HEAD_DIM + SUBLANES, :] + jnp.concatenate(terms, axis=1)
            n_t = o_t[:HEAD_DIM, :] * jnp.concatenate([1.0 / den] * (HEAD_DIM // SUBLANES), axis=0)
            for j in range(PAIRS_PER_KV):
                pair = jnp.concatenate([n_t[:, 2 * j * WINDOW:(2 * j + 1) * WINDOW],
                                        n_t[:, (2 * j + 1) * WINDOW:(2 * j + 2) * WINDOW]], axis=0)
                out_chunks[kv * PAIRS_PER_KV + j][qb] = pair.T

        z_cols = max(2 * LANES, D_MODEL // len(units))
        n_zc = D_MODEL // z_cols
        z_rows = tc * n_zc // len(units)
        z_parts = [[None] * n_zc for _ in range(tc // z_rows)]

        def gate_piece(i):
            z0, c0 = (i // n_zc) * z_rows, (i % n_zc) * z_cols
            z_parts[i // n_zc][i % n_zc] = (_dot(y[z0:z0 + z_rows, :], w_z_ref[:, c0:c0 + z_cols])
                                            + b_in_ref[:, qkv_w + c0:qkv_w + c0 + z_cols])

        gate_piece(0)
        st_next = scores(*units[0])
        for i, unit in enumerate(units):
            st = st_next
            if i + 1 < len(units):
                gate_piece(i + 1)
                st_next = scores(*units[i + 1])
            attend(*unit, st)

        z = jnp.concatenate([jnp.concatenate(row, axis=1) for row in z_parts], axis=0)
        attn = jnp.concatenate([jnp.concatenate(col, axis=0) for col in out_chunks], axis=1)
        gated = (attn * _silu(z)).astype(BF16)
        for p0 in range(0, tc, OUT_ROWS):
            h2 = h[p0:p0 + OUT_ROWS, :] + _dot(gated[p0:p0 + OUT_ROWS, :], w_out_ref[:, 0:D_MODEL]) + b_out_ref[...]
            o_ref[0, r0 + p0:r0 + p0 + OUT_ROWS, :] = _rms_norm(h2, gf_ref[...])

    for c in range(tm // tc):
        chunk(c)
    kx[0:WINDOW, :] = kx[tm:tm + WINDOW, :]
    vtx[:, 0:WINDOW] = vtx[:, tm:tm + WINDOW]


def _odd_layer(h, positions, g, w_in, b_in, sinks, w_out, b_out, g_final, *, tm, tc):
    b, s, d = h.shape
    tile = pl.BlockSpec((1, tm, d), lambda i, j: (i, j, 0))
    pos_spec = pl.BlockSpec((1, 1, tm), lambda i, j: (i, 0, j))
    qkv_w = d + 2 * KV_WIDTH
    w_qkv_t = w_in[:, 0:qkv_w].T
    b_qkv = jnp.broadcast_to(b_in[0:qkv_w, None], (qkv_w, LANES))
    z_shape = (w_in.shape[0], w_in.shape[1] - qkv_w)
    weight_bytes = 2 * (w_in.size + w_out.size)
    stages = [_stage_shape(shape) for shape in (w_qkv_t.shape, z_shape, w_out.shape)]
    stage_bytes = sum(4 * math.prod(st) for st in stages)
    tile_bytes = 4 * tm * d
    vmem_limit = weight_bytes + stage_bytes + (4 + 12) * tile_bytes + (8 << 20)
    hbm = pl.BlockSpec(memory_space=pl.ANY)
    return pl.pallas_call(
        functools.partial(_odd_kernel, tm=tm, tc=tc),
        grid=(b, s // tm),
        in_specs=[pl.BlockSpec(memory_space=pltpu.SMEM), tile, pos_spec, _const_spec((1, d)),
                  hbm, _const_spec((qkv_w, LANES)), hbm, _const_spec((1, ODD_IN)), hbm, _const_spec((1, d)),
                  _const_spec((1, d))],
        out_specs=tile,
        out_shape=jax.ShapeDtypeStruct(h.shape, h.dtype),
        scratch_shapes=[pltpu.VMEM((WINDOW + tm, LANES), BF16),
                        pltpu.VMEM((KV_WIDTH, WINDOW + tm), BF16),
                        pltpu.VMEM(_padded(w_qkv_t.shape), BF16),
                        pltpu.VMEM(_padded(z_shape), BF16),
                        pltpu.VMEM(_padded(w_out.shape), BF16),
                        *[pltpu.VMEM(st, F32) for st in stages],
                        *[pltpu.SemaphoreType.DMA((st[0],)) for st in stages]],
        compiler_params=pltpu.CompilerParams(
            dimension_semantics=("arbitrary", "arbitrary"), vmem_limit_bytes=vmem_limit),
        name="odd_layer",
    )(sinks, h, positions.reshape(b, 1, s), g.reshape(1, d), w_qkv_t, b_qkv, w_in, b_in.reshape(1, ODD_IN), w_out,
      b_out.reshape(1, d), g_final.reshape(1, d))


EVEN_TM = 1024
EVEN_TC = 256
ODD_TM = 1024
ODD_TC = 512


def kernel(x, positions, norm_g, w_in_even, w_pool, pool_scale, conv_w, w_out_even, w_in_odd, b_in_odd,
           attn_sinks, w_out_odd, b_out_odd, final_norm_g):
    assert norm_g.shape[0] == 2 and w_in_even.shape[0] == 1 and w_in_odd.shape[0] == 1
    h = _even_layer(x, norm_g[0], w_in_even[0], w_pool[0], pool_scale[0], conv_w[0], w_out_even[0],
                    tm=EVEN_TM, tc=EVEN_TC)
    return _odd_layer(h, positions, norm_g[1], w_in_odd[0], b_in_odd[0], attn_sinks[0], w_out_odd[0], b_out_odd[0],
                      final_norm_g, tm=ODD_TM, tc=ODD_TC)
```

```python
import functools
import math

import jax
import jax.numpy as jnp
from jax import lax
from jax.experimental import pallas as pl
from jax.experimental.pallas import tpu as pltpu

D_MODEL = 1024
NORM_EPS = 1e-5
POOL_WINDOWS = (2, 4, 8, 16)
POOL_GC = D_MODEL // len(POOL_WINDOWS)
CONV_WIDTH = D_MODEL
EVEN_IN = 6 * D_MODEL
HEAD_DIM = 64
N_HEADS = 16
N_KV_HEADS = 2
GROUP = N_HEADS // N_KV_HEADS
KV_WIDTH = N_KV_HEADS * HEAD_DIM
ODD_IN = 2 * D_MODEL + 2 * KV_WIDTH
WINDOW = 128
ROPE_THETA = 500000.0
ROT_DIMS = HEAD_DIM // 4
N_FREQ = ROT_DIMS // 2

VMEM_BYTES = 64 << 20
LANES = 128
SUBLANES = 8
POOL_CARRY = 32
CONV_CARRY = SUBLANES
PAIRS_PER_KV = GROUP // 2
LOG2E = math.log2(math.e)
Q_SCALE = HEAD_DIM ** -0.5 * LOG2E
MASK_BIAS = 2.0 ** 100
STAGE_SLOTS = 4
STAGE_BLOCK_BYTES = 1 << 20
OUT_ROWS = 512
GATE_COLS = 256

_INV_FREQ = tuple(float(ROPE_THETA ** (-(2.0 * i) / ROT_DIMS)) for i in range(N_FREQ))

F32 = jnp.float32
BF16 = jnp.bfloat16


def _dot(a, b):
    return jnp.dot(a, b, preferred_element_type=F32)


def _rms_norm(x, g):
    ms = jnp.mean(x * x, axis=-1, keepdims=True)
    return x * lax.rsqrt(ms + NORM_EPS) * g


def _silu(z):
    return z * jax.nn.sigmoid(z)


def _rope_angles(pos_row, tm):
    sub = lax.broadcasted_iota(jnp.int32, (N_FREQ, tm), 0)
    inv_freq = jnp.full((N_FREQ, tm), _INV_FREQ[N_FREQ - 1], F32)
    for f in range(N_FREQ - 1):
        inv_freq = jnp.where(sub == f, _INV_FREQ[f], inv_freq)
    ang = pos_row.astype(F32) * inv_freq
    return jnp.cos(ang), jnp.sin(ang)


def _stage_shape(weight_shape):
    k_rows, width = weight_shape
    rows = min(k_rows, pl.next_power_of_2(STAGE_BLOCK_BYTES // (4 * width) + 1) // 2)
    return (min(STAGE_SLOTS, k_rows // rows), rows, width)


def _load_weight(src_hbm, dst, stage, sem, col0=0):
    n_slots, rows, width = stage.shape
    n_blocks = src_hbm.shape[0] // rows

    def copy(k):
        return pltpu.make_async_copy(src_hbm.at[pl.ds(k * rows, rows), pl.ds(col0, width)], stage.at[k % n_slots],
                                     sem.at[k % n_slots])

    for k in range(min(n_slots - 1, n_blocks)):
        copy(k).start()
    for k in range(n_blocks):
        if k + n_slots - 1 < n_blocks:
            copy(k + n_slots - 1).start()
        copy(k).wait()
        dst[k * rows:(k + 1) * rows, 0:width] = stage[k % n_slots].astype(BF16)


def _first_step():
    return (pl.program_id(0) == 0) & (pl.program_id(1) == 0)


def _even_kernel(x_ref, g_ref, w_in_hbm, w_pool_hbm, ps_ref, conv_ref, w_out_hbm, o_ref,
                 ubuf, s2buf, s4buf, cbuf, w_in_ref, w_pool_ref, w_out_ref, stage_in, stage_pool, stage_out,
                 sem_in, sem_pool, sem_out, *, tm, tc):
    t = pl.program_id(1)
    n_slabs = D_MODEL // LANES
    slabs_per_group = POOL_GC // LANES
    n_s2, n_s4 = n_slabs - slabs_per_group, n_slabs - 2 * slabs_per_group
    ext = POOL_CARRY + tc

    @pl.when(_first_step())
    def _():
        _load_weight(w_in_hbm, w_in_ref, stage_in, sem_in)
        _load_weight(w_out_hbm, w_out_ref, stage_out, sem_out)
        _load_weight(w_pool_hbm, w_pool_ref, stage_pool, sem_pool)

    @pl.when(t == 0)
    def _():
        ubuf[:, 0:POOL_CARRY, :] = jnp.zeros((n_slabs, POOL_CARRY, LANES), F32)
        cbuf[:, 0:CONV_CARRY, :] = jnp.zeros((n_slabs, CONV_CARRY, LANES), F32)

    def chunk(c):
        r0 = c * tc
        x = x_ref[0, r0:r0 + tc, :]
        y = _rms_norm(x, g_ref[...]).astype(BF16)
        u = _dot(y, w_in_ref[:, 0:D_MODEL])
        gate_c = _dot(y, w_in_ref[:, 2 * D_MODEL:3 * D_MODEL])
        h_c = _dot(y, w_in_ref[:, 3 * D_MODEL:4 * D_MODEL])
        gate_b = _dot(y, w_in_ref[:, D_MODEL:2 * D_MODEL])
        z_b = _dot(y, w_in_ref[:, 5 * D_MODEL:6 * D_MODEL])
        z_a = _dot(y, w_in_ref[:, 4 * D_MODEL:5 * D_MODEL])

        pos = t * tm + r0 + lax.broadcasted_iota(jnp.int32, (tc, LANES), 0)
        a_parts = []
        for g, w in enumerate(POOL_WINDOWS):
            inv = 1.0 / jnp.minimum(pos + 1, w).astype(F32)
            slabs = []
            for j in range(g * slabs_per_group, (g + 1) * slabs_per_group):
                u_j = u[:, j * LANES:(j + 1) * LANES]
                ubuf[j, r0 + POOL_CARRY:r0 + ext, :] = u_j
                s = ubuf[j, r0 + 8:r0 + ext, :] + ubuf[j, r0 + 7:r0 + ext - 1, :]
                if w >= 4:
                    jj = c * n_s2 + j - slabs_per_group
                    s2buf[jj, :, :] = s
                    s = s[8:, :] + s2buf[jj, 6:ext - 10, :]
                if w >= 8:
                    jj = c * n_s4 + j - 2 * slabs_per_group
                    s4buf[jj, :, :] = s
                    s = s[8:, :] + s4buf[jj, 4:ext - 20, :]
                if w >= 16:
                    s = s[8:, :] + s[:-8, :]
                slabs.append(s[s.shape[0] - tc:, :] * inv - u_j)
            pooled = jnp.concatenate(slabs, axis=1).astype(BF16)
            a_parts.append(_dot(pooled, w_pool_ref[g * POOL_GC:(g + 1) * POOL_GC, :]))
        a = jnp.concatenate(a_parts, axis=1) * ps_ref[...]

        cu = gate_c * h_c
        cw = conv_ref[...]
        v_parts = []
        for j in range(n_slabs):
            lanes = slice(j * LANES, (j + 1) * LANES)
            cbuf[j, r0 + CONV_CARRY:r0 + CONV_CARRY + tc, :] = cu[:, lanes]
            v_parts.append(cw[2:3, lanes] * cu[:, lanes]
                           + cw[1:2, lanes] * cbuf[j, r0 + CONV_CARRY - 1:r0 + CONV_CARRY - 1 + tc, :]
                           + cw[0:1, lanes] * cbuf[j, r0 + CONV_CARRY - 2:r0 + CONV_CARRY - 2 + tc, :])
        bo = gate_b * jnp.concatenate(v_parts, axis=1)

        mix_b = (bo * _silu(z_b)).astype(BF16)
        mix_a = (a * _silu(z_a)).astype(BF16)
        o_ref[0, r0:r0 + tc, :] = x + (_dot(mix_b, w_out_ref[D_MODEL:, 0:D_MODEL])
                                       + _dot(mix_a, w_out_ref[0:D_MODEL, 0:D_MODEL]))

    for c in range(tm // tc):
        chunk(c)
    ubuf[:, 0:POOL_CARRY, :] = ubuf[:, tm:tm + POOL_CARRY, :]
    cbuf[:, 0:CONV_CARRY, :] = cbuf[:, tm:tm + CONV_CARRY, :]


def _const_spec(shape):
    nd = len(shape)
    return pl.BlockSpec(shape, lambda *_: (0,) * nd, pipeline_mode=pl.Buffered(1))


def _padded(shape):
    k_rows, width = shape
    return (k_rows, width + LANES) if (width // LANES) % 2 == 0 else (k_rows, width)


def _even_layer(x, g, w_in, w_pool, pool_scale, conv_w, w_out, *, tm, tc):
    b, s, d = x.shape
    tile = pl.BlockSpec((1, tm, d), lambda i, j: (i, j, 0))
    n_slabs, per_group = d // LANES, POOL_GC // LANES
    w_pool = w_pool.reshape(-1, w_pool.shape[-1])
    weight_bytes = 2 * (w_in.size + w_pool.size + w_out.size)
    stages = [_stage_shape(w.shape) for w in (w_in, w_pool, w_out)]
    stage_bytes = sum(4 * math.prod(st) for st in stages)
    tile_bytes, chunk_bytes = 4 * tm * d, 4 * tc * d
    assert weight_bytes + stage_bytes + (4 + 2) * tile_bytes + (3 + 10) * chunk_bytes <= VMEM_BYTES
    vmem_limit = VMEM_BYTES
    hbm = pl.BlockSpec(memory_space=pl.ANY)
    return pl.pallas_call(
        functools.partial(_even_kernel, tm=tm, tc=tc),
        grid=(b, s // tm),
        in_specs=[tile, _const_spec((1, d)), hbm, hbm, _const_spec((1, d)), _const_spec(conv_w.shape), hbm],
        out_specs=tile,
        out_shape=jax.ShapeDtypeStruct(x.shape, x.dtype),
        scratch_shapes=[pltpu.VMEM((n_slabs, POOL_CARRY + tm, LANES), F32),
                        pltpu.VMEM((tm // tc * (n_slabs - per_group), POOL_CARRY + tc - 8, LANES), F32),
                        pltpu.VMEM((tm // tc * (n_slabs - 2 * per_group), POOL_CARRY + tc - 16, LANES), F32),
                        pltpu.VMEM((n_slabs, CONV_CARRY + tm, LANES), F32),
                        pltpu.VMEM(_padded(w_in.shape), BF16),
                        pltpu.VMEM(w_pool.shape, BF16),
                        pltpu.VMEM(_padded(w_out.shape), BF16),
                        *[pltpu.VMEM(st, F32) for st in stages],
                        *[pltpu.SemaphoreType.DMA((st[0],)) for st in stages]],
        compiler_params=pltpu.CompilerParams(
            dimension_semantics=("arbitrary", "arbitrary"), vmem_limit_bytes=vmem_limit),
        name="even_layer",
    )(x, g.reshape(1, d), w_in, w_pool, pool_scale.reshape(1, d), conv_w, w_out)


def _odd_kernel(sink_ref, h_ref, pos_ref, g_ref, w_qkv_hbm, b_qkv_ref, w_in_hbm, b_in_ref, w_out_hbm, b_out_ref, gf_ref,
                o_ref, kx, vtx, w_qkv_ref, w_z_ref, w_out_ref, stage_qkv, stage_z, stage_out, sem_qkv, sem_z, sem_out,
                *, tm, tc):
    t = pl.program_id(1)
    nq = tc // WINDOW
    qkv_w = D_MODEL + 2 * KV_WIDTH

    @pl.when(_first_step())
    def _():
        _load_weight(w_qkv_hbm, w_qkv_ref, stage_qkv, sem_qkv)
        _load_weight(w_in_hbm, w_z_ref, stage_z, sem_z, col0=qkv_w)
        _load_weight(w_out_hbm, w_out_ref, stage_out, sem_out)

    @pl.when(t == 0)
    def _():
        kx[0:WINDOW, :] = jnp.zeros((WINDOW, LANES), BF16)
        vtx[:, 0:WINDOW] = jnp.zeros((KV_WIDTH, WINDOW), BF16)

    slot = lax.broadcasted_iota(jnp.int32, (WINDOW, LANES), 0)
    qidx = lax.broadcasted_iota(jnp.int32, (WINDOW, LANES), 1)
    diff = slot - qidx
    eye = jnp.where(diff == 0, 1.0, 0.0).astype(BF16)
    masked = jnp.full((WINDOW, LANES), -MASK_BIAS, F32)
    bias_prev = jnp.where(diff > 0, 0.0, masked)
    bias_cur = jnp.where(diff <= 0, 0.0, masked)
    bias_mid = jnp.concatenate([bias_prev, bias_cur], axis=0).astype(BF16)
    bias_first = jnp.concatenate([jnp.where(t == 0, masked, bias_prev), bias_cur], axis=0).astype(BF16)
    ones_rows = jnp.ones((2 * SUBLANES, 2 * WINDOW), BF16)
    no_head = jnp.zeros((HEAD_DIM, WINDOW), BF16)

    def rotary(x_t, cos8, sin8, scale):
        x1, x2 = x_t[0:N_FREQ, :], x_t[N_FREQ:ROT_DIMS, :]
        rest = x_t[ROT_DIMS:, :]
        return [x1 * cos8 - x2 * sin8, x2 * cos8 + x1 * sin8, rest if scale == 1.0 else rest * scale]

    def chunk(c):
        r0 = c * tc
        h = h_ref[0, r0:r0 + tc, :]
        y = _rms_norm(h, g_ref[...]).astype(BF16)
        qkv_t = lax.dot_general(w_qkv_ref[:, 0:D_MODEL], y, (((1,), (1,)), ((), ())), preferred_element_type=F32)
        qkv_t = qkv_t + jnp.concatenate([b_qkv_ref[...]] * (tc // LANES), axis=1)

        cos8, sin8 = _rope_angles(pos_ref[0, :, r0:r0 + tc], tc)
        cq8, sq8 = cos8 * Q_SCALE, sin8 * Q_SCALE
        q_rows = []
        for hd in range(N_HEADS):
            q_rows += rotary(qkv_t[hd * HEAD_DIM:(hd + 1) * HEAD_DIM, :], cq8, sq8, Q_SCALE)
        qr_t = jnp.concatenate(q_rows, axis=0).astype(BF16)
        k_rows = []
        for kv in range(N_KV_HEADS):
            k_rows += rotary(qkv_t[D_MODEL + kv * HEAD_DIM:D_MODEL + (kv + 1) * HEAD_DIM, :], cos8, sin8, 1.0)
        kr_t = jnp.concatenate(k_rows, axis=0)
        k0, k1 = WINDOW + r0, WINDOW + r0 + tc
        for blk in range(nq):
            kx[k0 + blk * WINDOW:k0 + (blk + 1) * WINDOW, :] = kr_t[:, blk * WINDOW:(blk + 1) * WINDOW].T.astype(BF16)
        vtx[:, k0:k1] = qkv_t[D_MODEL + KV_WIDTH:qkv_w, :].astype(BF16)

        out_chunks = [[None] * nq for _ in range(D_MODEL // LANES)]
        units = [(qb, kv) for qb in range(nq) for kv in range(N_KV_HEADS)]

        def scores(qb, kv):
            a0 = r0 + qb * WINDOW
            bias = bias_first if a0 == 0 else bias_mid
            w_ext = jnp.concatenate([kx[a0:a0 + 2 * WINDOW, :], bias], axis=1)
            cols = []
            for g in range(GROUP):
                hd = kv * GROUP + g
                q_h = qr_t[hd * HEAD_DIM:(hd + 1) * HEAD_DIM, qb * WINDOW:(qb + 1) * WINDOW]
                cols.append(jnp.concatenate([q_h, no_head, eye] if kv == 0 else [no_head, q_h, eye], axis=0))
            return _dot(w_ext, jnp.concatenate(cols, axis=1))

        def attend(qb, kv, st):
            a0 = r0 + qb * WINDOW
            blocks = []
            terms = []
            for g in range(GROUP):
                sink = sink_ref[kv * GROUP + g] * LOG2E
                blk_s = st[:, g * WINDOW:(g + 1) * WINDOW]
                m = jnp.maximum(jnp.max(blk_s, axis=0, keepdims=True), sink)
                blocks.append(jnp.exp2(blk_s - m).astype(BF16))
                terms.append(jnp.exp2(sink - m))
            p_t = jnp.concatenate(blocks, axis=1)
            v_win = vtx[kv * HEAD_DIM:(kv + 1) * HEAD_DIM, a0:a0 + 2 * WINDOW]
            o_t = _dot(jnp.concatenate([v_win, ones_rows], axis=0), p_t)
            den = o_t[HEAD_DIM:HEAD_DIM + SUBLANES, :] + jnp.concatenate(terms, axis=1)
            n_t = o_t[:HEAD_DIM, :] * jnp.concatenate([1.0 / den] * (HEAD_DIM // SUBLANES), axis=0)
            for j in range(PAIRS_PER_KV):
                pair = jnp.concatenate([n_t[:, 2 * j * WINDOW:(2 * j + 1) * WINDOW],
                                        n_t[:, (2 * j + 1) * WINDOW:(2 * j + 2) * WINDOW]], axis=0)
                out_chunks[kv * PAIRS_PER_KV + j][qb] = pair.T

        n_pieces = D_MODEL // GATE_COLS
        every = max(1, len(units) // n_pieces)
        z_parts = []

        def gate_piece():
            c0 = len(z_parts) * GATE_COLS
            z_parts.append(_dot(y, w_z_ref[:, c0:c0 + GATE_COLS]) + b_in_ref[:, qkv_w + c0:qkv_w + c0 + GATE_COLS])

        gate_piece()
        st_next = scores(*units[0])
        for i, unit in enumerate(units):
            st = st_next
            if i + 1 < len(units):
                if (i + 1) % every == 0 and len(z_parts) < n_pieces:
                    gate_piece()
                st_next = scores(*units[i + 1])
            attend(*unit, st)
        while len(z_parts) < n_pieces:
            gate_piece()

        z = jnp.concatenate(z_parts, axis=1)
        attn = jnp.concatenate([jnp.concatenate(col, axis=0) for col in out_chunks], axis=1)
        gated = (attn * _silu(z)).astype(BF16)
        for p0 in range(0, tc, OUT_ROWS):
            h2 = h[p0:p0 + OUT_ROWS, :] + _dot(gated[p0:p0 + OUT_ROWS, :], w_out_ref[:, 0:D_MODEL]) + b_out_ref[...]
            o_ref[0, r0 + p0:r0 + p0 + OUT_ROWS, :] = _rms_norm(h2, gf_ref[...])

    for c in range(tm // tc):
        chunk(c)
    kx[0:WINDOW, :] = kx[tm:tm + WINDOW, :]
    vtx[:, 0:WINDOW] = vtx[:, tm:tm + WINDOW]


def _odd_layer(h, positions, g, w_in, b_in, sinks, w_out, b_out, g_final, *, tm, tc):
    b, s, d = h.shape
    tile = pl.BlockSpec((1, tm, d), lambda i, j: (i, j, 0))
    pos_spec = pl.BlockSpec((1, 1, tm), lambda i, j: (i, 0, j))
    qkv_w = d + 2 * KV_WIDTH
    w_qkv_t = w_in[:, 0:qkv_w].T
    b_qkv = jnp.broadcast_to(b_in[0:qkv_w, None], (qkv_w, LANES))
    z_shape = (w_in.shape[0], w_in.shape[1] - qkv_w)
    weight_bytes = 2 * (w_in.size + w_out.size)
    stages = [_stage_shape(shape) for shape in (w_qkv_t.shape, z_shape, w_out.shape)]
    stage_bytes = sum(4 * math.prod(st) for st in stages)
    tile_bytes = 4 * tm * d
    vmem_limit = weight_bytes + stage_bytes + (4 + 12) * tile_bytes + (8 << 20)
    hbm = pl.BlockSpec(memory_space=pl.ANY)
    return pl.pallas_call(
        functools.partial(_odd_kernel, tm=tm, tc=tc),
        grid=(b, s // tm),
        in_specs=[pl.BlockSpec(memory_space=pltpu.SMEM), tile, pos_spec, _const_spec((1, d)),
                  hbm, _const_spec((qkv_w, LANES)), hbm, _const_spec((1, ODD_IN)), hbm, _const_spec((1, d)),
                  _const_spec((1, d))],
        out_specs=tile,
        out_shape=jax.ShapeDtypeStruct(h.shape, h.dtype),
        scratch_shapes=[pltpu.VMEM((WINDOW + tm, LANES), BF16),
                        pltpu.VMEM((KV_WIDTH, WINDOW + tm), BF16),
                        pltpu.VMEM(_padded(w_qkv_t.shape), BF16),
                        pltpu.VMEM(_padded(z_shape), BF16),
                        pltpu.VMEM(_padded(w_out.shape), BF16),
                        *[pltpu.VMEM(st, F32) for st in stages],
                        *[pltpu.SemaphoreType.DMA((st[0],)) for st in stages]],
        compiler_params=pltpu.CompilerParams(
            dimension_semantics=("arbitrary", "arbitrary"), vmem_limit_bytes=vmem_limit),
        name="odd_layer",
    )(sinks, h, positions.reshape(b, 1, s), g.reshape(1, d), w_qkv_t, b_qkv, w_in, b_in.reshape(1, ODD_IN), w_out,
      b_out.reshape(1, d), g_final.reshape(1, d))


EVEN_TM = 1024
EVEN_TC = 256
ODD_TM = 1024
ODD_TC = 512


def kernel(x, positions, norm_g, w_in_even, w_pool, pool_scale, conv_w, w_out_even, w_in_odd, b_in_odd,
           attn_sinks, w_out_odd, b_out_odd, final_norm_g):
    assert norm_g.shape[0] == 2 and w_in_even.shape[0] == 1 and w_in_odd.shape[0] == 1
    h = _even_layer(x, norm_g[0], w_in_even[0], w_pool[0], pool_scale[0], conv_w[0], w_out_even[0],
                    tm=EVEN_TM, tc=EVEN_TC)
    return _odd_layer(h, positions, norm_g[1], w_in_odd[0], b_in_odd[0], attn_sinks[0], w_out_odd[0], b_out_odd[0],
                      final_norm_g, tm=ODD_TM, tc=ODD_TC)
```

```python
import functools
import math

import jax
import jax.numpy as jnp
from jax import lax
from jax.experimental import pallas as pl
from jax.experimental.pallas import tpu as pltpu

D_MODEL = 1024
NORM_EPS = 1e-5
POOL_WINDOWS = (2, 4, 8, 16)
POOL_GC = D_MODEL // len(POOL_WINDOWS)
CONV_WIDTH = D_MODEL
EVEN_IN = 6 * D_MODEL
HEAD_DIM = 64
N_HEADS = 16
N_KV_HEADS = 2
GROUP = N_HEADS // N_KV_HEADS
KV_WIDTH = N_KV_HEADS * HEAD_DIM
ODD_IN = 2 * D_MODEL + 2 * KV_WIDTH
WINDOW = 128
ROPE_THETA = 500000.0
ROT_DIMS = HEAD_DIM // 4
N_FREQ = ROT_DIMS // 2

VMEM_BYTES = 64 << 20
LANES = 128
SUBLANES = 8
POOL_CARRY = 32
CONV_CARRY = SUBLANES
PAIRS_PER_KV = GROUP // 2
LOG2E = math.log2(math.e)
Q_SCALE = HEAD_DIM ** -0.5 * LOG2E
MASK_BIAS = 2.0 ** 100
STAGE_SLOTS = 4
STAGE_BLOCK_BYTES = 1 << 20
OUT_ROWS = 512
GATE_COLS = 256

_INV_FREQ = tuple(float(ROPE_THETA ** (-(2.0 * i) / ROT_DIMS)) for i in range(N_FREQ))

F32 = jnp.float32
BF16 = jnp.bfloat16


def _dot(a, b):
    return jnp.dot(a, b, preferred_element_type=F32)


def _rms_norm(x, g):
    ms = jnp.mean(x * x, axis=-1, keepdims=True)
    return x * lax.rsqrt(ms + NORM_EPS) * g


def _silu(z):
    return z * jax.nn.sigmoid(z)


def _rope_angles(pos_row, tm):
    sub = lax.broadcasted_iota(jnp.int32, (N_FREQ, tm), 0)
    inv_freq = jnp.full((N_FREQ, tm), _INV_FREQ[N_FREQ - 1], F32)
    for f in range(N_FREQ - 1):
        inv_freq = jnp.where(sub == f, _INV_FREQ[f], inv_freq)
    ang = pos_row.astype(F32) * inv_freq
    return jnp.cos(ang), jnp.sin(ang)


def _stage_shape(weight_shape):
    k_rows, width = weight_shape
    rows = min(k_rows, pl.next_power_of_2(STAGE_BLOCK_BYTES // (4 * width) + 1) // 2)
    return (min(STAGE_SLOTS, k_rows // rows), rows, width)


def _load_weight(src_hbm, dst, stage, sem, col0=0, transpose=False):
    n_slots, rows, width = stage.shape
    n_blocks = src_hbm.shape[0] // rows

    def copy(k):
        return pltpu.make_async_copy(src_hbm.at[pl.ds(k * rows, rows), pl.ds(col0, width)], stage.at[k % n_slots],
                                     sem.at[k % n_slots])

    for k in range(min(n_slots - 1, n_blocks)):
        copy(k).start()
    for k in range(n_blocks):
        if k + n_slots - 1 < n_blocks:
            copy(k + n_slots - 1).start()
        copy(k).wait()
        if not transpose:
            dst[k * rows:(k + 1) * rows, 0:width] = stage[k % n_slots].astype(BF16)
            continue
        for r in range(0, rows, LANES):
            for c in range(0, width, LANES):
                dst[c:c + LANES, k * rows + r:k * rows + r + LANES] = (
                    stage[k % n_slots, r:r + LANES, c:c + LANES].T.astype(BF16))


def _first_step():
    return (pl.program_id(0) == 0) & (pl.program_id(1) == 0)


def _even_kernel(x_ref, g_ref, w_in_hbm, w_pool_hbm, ps_ref, conv_ref, w_out_hbm, o_ref,
                 ubuf, s2buf, s4buf, cbuf, w_in_ref, w_pool_ref, w_out_ref, stage_in, stage_pool, stage_out,
                 sem_in, sem_pool, sem_out, *, tm, tc):
    t = pl.program_id(1)
    n_slabs = D_MODEL // LANES
    slabs_per_group = POOL_GC // LANES
    n_s2, n_s4 = n_slabs - slabs_per_group, n_slabs - 2 * slabs_per_group
    ext = POOL_CARRY + tc

    @pl.when(_first_step())
    def _():
        _load_weight(w_in_hbm, w_in_ref, stage_in, sem_in)
        _load_weight(w_out_hbm, w_out_ref, stage_out, sem_out)
        _load_weight(w_pool_hbm, w_pool_ref, stage_pool, sem_pool)

    @pl.when(t == 0)
    def _():
        ubuf[:, 0:POOL_CARRY, :] = jnp.zeros((n_slabs, POOL_CARRY, LANES), F32)
        cbuf[:, 0:CONV_CARRY, :] = jnp.zeros((n_slabs, CONV_CARRY, LANES), F32)

    def chunk(c):
        r0 = c * tc
        x = x_ref[0, r0:r0 + tc, :]
        y = _rms_norm(x, g_ref[...]).astype(BF16)
        u = _dot(y, w_in_ref[:, 0:D_MODEL])
        gate_c = _dot(y, w_in_ref[:, 2 * D_MODEL:3 * D_MODEL])
        h_c = _dot(y, w_in_ref[:, 3 * D_MODEL:4 * D_MODEL])
        gate_b = _dot(y, w_in_ref[:, D_MODEL:2 * D_MODEL])
        z_b = _dot(y, w_in_ref[:, 5 * D_MODEL:6 * D_MODEL])
        z_a = _dot(y, w_in_ref[:, 4 * D_MODEL:5 * D_MODEL])

        pos = t * tm + r0 + lax.broadcasted_iota(jnp.int32, (tc, LANES), 0)
        a_parts = []
        for g, w in enumerate(POOL_WINDOWS):
            inv = 1.0 / jnp.minimum(pos + 1, w).astype(F32)
            slabs = []
            for j in range(g * slabs_per_group, (g + 1) * slabs_per_group):
                u_j = u[:, j * LANES:(j + 1) * LANES]
                ubuf[j, r0 + POOL_CARRY:r0 + ext, :] = u_j
                s = ubuf[j, r0 + 8:r0 + ext, :] + ubuf[j, r0 + 7:r0 + ext - 1, :]
                if w >= 4:
                    jj = c * n_s2 + j - slabs_per_group
                    s2buf[jj, :, :] = s
                    s = s[8:, :] + s2buf[jj, 6:ext - 10, :]
                if w >= 8:
                    jj = c * n_s4 + j - 2 * slabs_per_group
                    s4buf[jj, :, :] = s
                    s = s[8:, :] + s4buf[jj, 4:ext - 20, :]
                if w >= 16:
                    s = s[8:, :] + s[:-8, :]
                slabs.append(s[s.shape[0] - tc:, :] * inv - u_j)
            pooled = jnp.concatenate(slabs, axis=1).astype(BF16)
            a_parts.append(_dot(pooled, w_pool_ref[g * POOL_GC:(g + 1) * POOL_GC, :]))
        a = jnp.concatenate(a_parts, axis=1) * ps_ref[...]

        cu = gate_c * h_c
        cw = conv_ref[...]
        v_parts = []
        for j in range(n_slabs):
            lanes = slice(j * LANES, (j + 1) * LANES)
            cbuf[j, r0 + CONV_CARRY:r0 + CONV_CARRY + tc, :] = cu[:, lanes]
            v_parts.append(cw[2:3, lanes] * cu[:, lanes]
                           + cw[1:2, lanes] * cbuf[j, r0 + CONV_CARRY - 1:r0 + CONV_CARRY - 1 + tc, :]
                           + cw[0:1, lanes] * cbuf[j, r0 + CONV_CARRY - 2:r0 + CONV_CARRY - 2 + tc, :])
        bo = gate_b * jnp.concatenate(v_parts, axis=1)

        mix_b = (bo * _silu(z_b)).astype(BF16)
        mix_a = (a * _silu(z_a)).astype(BF16)
        o_ref[0, r0:r0 + tc, :] = x + (_dot(mix_b, w_out_ref[D_MODEL:, 0:D_MODEL])
                                       + _dot(mix_a, w_out_ref[0:D_MODEL, 0:D_MODEL]))

    for c in range(tm // tc):
        chunk(c)
    ubuf[:, 0:POOL_CARRY, :] = ubuf[:, tm:tm + POOL_CARRY, :]
    cbuf[:, 0:CONV_CARRY, :] = cbuf[:, tm:tm + CONV_CARRY, :]


def _const_spec(shape):
    nd = len(shape)
    return pl.BlockSpec(shape, lambda *_: (0,) * nd, pipeline_mode=pl.Buffered(1))


def _padded(shape):
    k_rows, width = shape
    return (k_rows, width + LANES) if (width // LANES) % 2 == 0 else (k_rows, width)


def _even_layer(x, g, w_in, w_pool, pool_scale, conv_w, w_out, *, tm, tc):
    b, s, d = x.shape
    tile = pl.BlockSpec((1, tm, d), lambda i, j: (i, j, 0))
    n_slabs, per_group = d // LANES, POOL_GC // LANES
    w_pool = w_pool.reshape(-1, w_pool.shape[-1])
    weight_bytes = 2 * (w_in.size + w_pool.size + w_out.size)
    stages = [_stage_shape(w.shape) for w in (w_in, w_pool, w_out)]
    stage_bytes = sum(4 * math.prod(st) for st in stages)
    tile_bytes, chunk_bytes = 4 * tm * d, 4 * tc * d
    assert weight_bytes + stage_bytes + (4 + 2) * tile_bytes + (3 + 10) * chunk_bytes <= VMEM_BYTES
    vmem_limit = VMEM_BYTES
    hbm = pl.BlockSpec(memory_space=pl.ANY)
    return pl.pallas_call(
        functools.partial(_even_kernel, tm=tm, tc=tc),
        grid=(b, s // tm),
        in_specs=[tile, _const_spec((1, d)), hbm, hbm, _const_spec((1, d)), _const_spec(conv_w.shape), hbm],
        out_specs=tile,
        out_shape=jax.ShapeDtypeStruct(x.shape, x.dtype),
        scratch_shapes=[pltpu.VMEM((n_slabs, POOL_CARRY + tm, LANES), F32),
                        pltpu.VMEM((tm // tc * (n_slabs - per_group), POOL_CARRY + tc - 8, LANES), F32),
                        pltpu.VMEM((tm // tc * (n_slabs - 2 * per_group), POOL_CARRY + tc - 16, LANES), F32),
                        pltpu.VMEM((n_slabs, CONV_CARRY + tm, LANES), F32),
                        pltpu.VMEM(_padded(w_in.shape), BF16),
                        pltpu.VMEM(w_pool.shape, BF16),
                        pltpu.VMEM(_padded(w_out.shape), BF16),
                        *[pltpu.VMEM(st, F32) for st in stages],
                        *[pltpu.SemaphoreType.DMA((st[0],)) for st in stages]],
        compiler_params=pltpu.CompilerParams(
            dimension_semantics=("arbitrary", "arbitrary"), vmem_limit_bytes=vmem_limit),
        name="even_layer",
    )(x, g.reshape(1, d), w_in, w_pool, pool_scale.reshape(1, d), conv_w, w_out)


def _odd_kernel(sink_ref, h_ref, pos_ref, g_ref, w_in_hbm, b_in_ref, w_out_hbm, b_out_ref, gf_ref,
                o_ref, kx, vtx, b_qkv_ref, w_qkv_ref, w_z_ref, w_out_ref, stage_qkv, stage_z, stage_out,
                sem_qkv, sem_z, sem_out, *, tm, tc):
    t = pl.program_id(1)
    nq = tc // WINDOW
    qkv_w = D_MODEL + 2 * KV_WIDTH

    @pl.when(_first_step())
    def _():
        for c0 in range(0, qkv_w, LANES):
            b_qkv_ref[c0:c0 + LANES, :] = jnp.broadcast_to(b_in_ref[:, c0:c0 + LANES], (LANES, LANES)).T
        _load_weight(w_in_hbm, w_qkv_ref, stage_qkv, sem_qkv, transpose=True)
        _load_weight(w_in_hbm, w_z_ref, stage_z, sem_z, col0=qkv_w)
        _load_weight(w_out_hbm, w_out_ref, stage_out, sem_out)

    @pl.when(t == 0)
    def _():
        kx[0:WINDOW, :] = jnp.zeros((WINDOW, LANES), BF16)
        vtx[:, 0:WINDOW] = jnp.zeros((KV_WIDTH, WINDOW), BF16)

    slot = lax.broadcasted_iota(jnp.int32, (WINDOW, LANES), 0)
    qidx = lax.broadcasted_iota(jnp.int32, (WINDOW, LANES), 1)
    diff = slot - qidx
    eye = jnp.where(diff == 0, 1.0, 0.0).astype(BF16)
    masked = jnp.full((WINDOW, LANES), -MASK_BIAS, F32)
    bias_prev = jnp.where(diff > 0, 0.0, masked)
    bias_cur = jnp.where(diff <= 0, 0.0, masked)
    bias_mid = jnp.concatenate([bias_prev, bias_cur], axis=0).astype(BF16)
    bias_first = jnp.concatenate([jnp.where(t == 0, masked, bias_prev), bias_cur], axis=0).astype(BF16)
    ones_rows = jnp.ones((2 * SUBLANES, 2 * WINDOW), BF16)
    no_head = jnp.zeros((HEAD_DIM, WINDOW), BF16)

    def rotary(x_t, cos8, sin8, scale):
        x1, x2 = x_t[0:N_FREQ, :], x_t[N_FREQ:ROT_DIMS, :]
        rest = x_t[ROT_DIMS:, :]
        return [x1 * cos8 - x2 * sin8, x2 * cos8 + x1 * sin8, rest if scale == 1.0 else rest * scale]

    def chunk(c):
        r0 = c * tc
        h = h_ref[0, r0:r0 + tc, :]
        y = _rms_norm(h, g_ref[...]).astype(BF16)
        qkv_t = lax.dot_general(w_qkv_ref[:, 0:D_MODEL], y, (((1,), (1,)), ((), ())), preferred_element_type=F32)
        qkv_t = qkv_t + jnp.concatenate([b_qkv_ref[...]] * (tc // LANES), axis=1)

        cos8, sin8 = _rope_angles(pos_ref[pl.ds(pl.program_id(0), 1), r0:r0 + tc], tc)
        cq8, sq8 = cos8 * Q_SCALE, sin8 * Q_SCALE
        q_rows = []
        for hd in range(N_HEADS):
            q_rows += rotary(qkv_t[hd * HEAD_DIM:(hd + 1) * HEAD_DIM, :], cq8, sq8, Q_SCALE)
        qr_t = jnp.concatenate(q_rows, axis=0).astype(BF16)
        k_rows = []
        for kv in range(N_KV_HEADS):
            k_rows += rotary(qkv_t[D_MODEL + kv * HEAD_DIM:D_MODEL + (kv + 1) * HEAD_DIM, :], cos8, sin8, 1.0)
        kr_t = jnp.concatenate(k_rows, axis=0)
        k0, k1 = WINDOW + r0, WINDOW + r0 + tc
        for blk in range(nq):
            kx[k0 + blk * WINDOW:k0 + (blk + 1) * WINDOW, :] = kr_t[:, blk * WINDOW:(blk + 1) * WINDOW].T.astype(BF16)
        vtx[:, k0:k1] = qkv_t[D_MODEL + KV_WIDTH:qkv_w, :].astype(BF16)

        out_chunks = [[None] * nq for _ in range(D_MODEL // LANES)]
        units = [(qb, kv) for qb in range(nq) for kv in range(N_KV_HEADS)]

        def scores(qb, kv):
            a0 = r0 + qb * WINDOW
            bias = bias_first if a0 == 0 else bias_mid
            w_ext = jnp.concatenate([kx[a0:a0 + 2 * WINDOW, :], bias], axis=1)
            cols = []
            for g in range(GROUP):
                hd = kv * GROUP + g
                q_h = qr_t[hd * HEAD_DIM:(hd + 1) * HEAD_DIM, qb * WINDOW:(qb + 1) * WINDOW]
                cols.append(jnp.concatenate([q_h, no_head, eye] if kv == 0 else [no_head, q_h, eye], axis=0))
            return _dot(w_ext, jnp.concatenate(cols, axis=1))

        def attend(qb, kv, st):
            a0 = r0 + qb * WINDOW
            blocks = []
            terms = []
            for g in range(GROUP):
                sink = sink_ref[kv * GROUP + g] * LOG2E
                blk_s = st[:, g * WINDOW:(g + 1) * WINDOW]
                m = jnp.maximum(jnp.max(blk_s, axis=0, keepdims=True), sink)
                blocks.append(jnp.exp2(blk_s - m).astype(BF16))
                terms.append(jnp.exp2(sink - m))
            p_t = jnp.concatenate(blocks, axis=1)
            v_win = vtx[kv * HEAD_DIM:(kv + 1) * HEAD_DIM, a0:a0 + 2 * WINDOW]
            o_t = _dot(jnp.concatenate([v_win, ones_rows], axis=0), p_t)
            den = o_t[HEAD_DIM:HEAD_DIM + SUBLANES, :] + jnp.concatenate(terms, axis=1)
            n_t = o_t[:HEAD_DIM, :] * jnp.concatenate([1.0 / den] * (HEAD_DIM // SUBLANES), axis=0)
            for j in range(PAIRS_PER_KV):
                pair = jnp.concatenate([n_t[:, 2 * j * WINDOW:(2 * j + 1) * WINDOW],
                                        n_t[:, (2 * j + 1) * WINDOW:(2 * j + 2) * WINDOW]], axis=0)
                out_chunks[kv * PAIRS_PER_KV + j][qb] = pair.T

        n_pieces = D_MODEL // GATE_COLS
        every = max(1, len(units) // n_pieces)
        z_parts = []

        def gate_piece():
            c0 = len(z_parts) * GATE_COLS
            z_parts.append(_dot(y, w_z_ref[:, c0:c0 + GATE_COLS]) + b_in_ref[:, qkv_w + c0:qkv_w + c0 + GATE_COLS])

        gate_piece()
        st_next = scores(*units[0])
        for i, unit in enumerate(units):
            st = st_next
            if i + 1 < len(units):
                if (i + 1) % every == 0 and len(z_parts) < n_pieces:
                    gate_piece()
                st_next = scores(*units[i + 1])
            attend(*unit, st)
        while len(z_parts) < n_pieces:
            gate_piece()

        z = jnp.concatenate(z_parts, axis=1)
        attn = jnp.concatenate([jnp.concatenate(col, axis=0) for col in out_chunks], axis=1)
        gated = (attn * _silu(z)).astype(BF16)
        for p0 in range(0, tc, OUT_ROWS):
            h2 = h[p0:p0 + OUT_ROWS, :] + _dot(gated[p0:p0 + OUT_ROWS, :], w_out_ref[:, 0:D_MODEL]) + b_out_ref[...]
            o_ref[0, r0 + p0:r0 + p0 + OUT_ROWS, :] = _rms_norm(h2, gf_ref[...])

    for c in range(tm // tc):
        chunk(c)
    kx[0:WINDOW, :] = kx[tm:tm + WINDOW, :]
    vtx[:, 0:WINDOW] = vtx[:, tm:tm + WINDOW]


def _odd_layer(h, positions, g, w_in, b_in, sinks, w_out, b_out, g_final, *, tm, tc):
    b, s, d = h.shape
    tile = pl.BlockSpec((1, tm, d), lambda i, j: (i, j, 0))
    pos_spec = pl.BlockSpec((b, tm), lambda i, j: (0, j))
    qkv_w = d + 2 * KV_WIDTH
    qkv_shape = (w_in.shape[0], qkv_w)
    z_shape = (w_in.shape[0], w_in.shape[1] - qkv_w)
    weight_bytes = 2 * (w_in.size + w_out.size)
    stages = [_stage_shape(shape) for shape in (qkv_shape, z_shape, w_out.shape)]
    stage_bytes = sum(4 * math.prod(st) for st in stages)
    tile_bytes = 4 * tm * d
    vmem_limit = weight_bytes + stage_bytes + (4 + 12) * tile_bytes + (8 << 20)
    hbm = pl.BlockSpec(memory_space=pl.ANY)
    return pl.pallas_call(
        functools.partial(_odd_kernel, tm=tm, tc=tc),
        grid=(b, s // tm),
        in_specs=[pl.BlockSpec(memory_space=pltpu.SMEM), tile, pos_spec, _const_spec((1, d)),
                  hbm, _const_spec((1, ODD_IN)), hbm, _const_spec((1, d)), _const_spec((1, d))],
        out_specs=tile,
        out_shape=jax.ShapeDtypeStruct(h.shape, h.dtype),
        scratch_shapes=[pltpu.VMEM((WINDOW + tm, LANES), BF16),
                        pltpu.VMEM((KV_WIDTH, WINDOW + tm), BF16),
                        pltpu.VMEM((qkv_w, LANES), F32),
                        pltpu.VMEM(_padded(qkv_shape[::-1]), BF16),
                        pltpu.VMEM(_padded(z_shape), BF16),
                        pltpu.VMEM(_padded(w_out.shape), BF16),
                        *[pltpu.VMEM(st, F32) for st in stages],
                        *[pltpu.SemaphoreType.DMA((st[0],)) for st in stages]],
        compiler_params=pltpu.CompilerParams(
            dimension_semantics=("arbitrary", "arbitrary"), vmem_limit_bytes=vmem_limit),
        name="odd_layer",
    )(sinks, h, positions, g.reshape(1, d), w_in, b_in.reshape(1, ODD_IN), w_out,
      b_out.reshape(1, d), g_final.reshape(1, d))


EVEN_TM = 1024
EVEN_TC = 256
ODD_TM = 1024
ODD_TC = 512


def kernel(x, positions, norm_g, w_in_even, w_pool, pool_scale, conv_w, w_out_even, w_in_odd, b_in_odd,
           attn_sinks, w_out_odd, b_out_odd, final_norm_g):
    assert norm_g.shape[0] == 2 and w_in_even.shape[0] == 1 and w_in_odd.shape[0] == 1
    h = _even_layer(x, norm_g[0], w_in_even[0], w_pool[0], pool_scale[0], conv_w[0], w_out_even[0],
                    tm=EVEN_TM, tc=EVEN_TC)
    return _odd_layer(h, positions, norm_g[1], w_in_odd[0], b_in_odd[0], attn_sinks[0], w_out_odd[0], b_out_odd[0],
                      final_norm_g, tm=ODD_TM, tc=ODD_TC)
```

```python
import functools
import math

import jax
import jax.numpy as jnp
from jax import lax
from jax.experimental import pallas as pl
from jax.experimental.pallas import tpu as pltpu

D_MODEL = 1024
NORM_EPS = 1e-5
POOL_WINDOWS = (2, 4, 8, 16)
POOL_GC = D_MODEL // len(POOL_WINDOWS)
CONV_WIDTH = D_MODEL
EVEN_IN = 6 * D_MODEL
HEAD_DIM = 64
N_HEADS = 16
N_KV_HEADS = 2
GROUP = N_HEADS // N_KV_HEADS
KV_WIDTH = N_KV_HEADS * HEAD_DIM
ODD_IN = 2 * D_MODEL + 2 * KV_WIDTH
WINDOW = 128
ROPE_THETA = 500000.0
ROT_DIMS = HEAD_DIM // 4
N_FREQ = ROT_DIMS // 2

VMEM_BYTES = 64 << 20
LANES = 128
SUBLANES = 8
POOL_CARRY = 32
CONV_CARRY = SUBLANES
PAIRS_PER_KV = GROUP // 2
LOG2E = math.log2(math.e)
Q_SCALE = HEAD_DIM ** -0.5 * LOG2E
MASK_BIAS = 2.0 ** 100
STAGE_SLOTS = 4
STAGE_BLOCK_BYTES = 1 << 20
OUT_ROWS = 512
GATE_COLS = 256

_INV_FREQ = tuple(float(ROPE_THETA ** (-(2.0 * i) / ROT_DIMS)) for i in range(N_FREQ))

F32 = jnp.float32
BF16 = jnp.bfloat16


def _dot(a, b):
    return jnp.dot(a, b, preferred_element_type=F32)


def _rms_norm(x, g):
    ms = jnp.mean(x * x, axis=-1, keepdims=True)
    return x * lax.rsqrt(ms + NORM_EPS) * g


def _silu(z):
    return z * jax.nn.sigmoid(z)


def _rope_angles(pos_row, tm):
    sub = lax.broadcasted_iota(jnp.int32, (N_FREQ, tm), 0)
    inv_freq = jnp.full((N_FREQ, tm), _INV_FREQ[N_FREQ - 1], F32)
    for f in range(N_FREQ - 1):
        inv_freq = jnp.where(sub == f, _INV_FREQ[f], inv_freq)
    ang = pos_row.astype(F32) * inv_freq
    return jnp.cos(ang), jnp.sin(ang)


def _stage_shape(weight_shape):
    k_rows, width = weight_shape
    rows = min(k_rows, pl.next_power_of_2(STAGE_BLOCK_BYTES // (4 * width) + 1) // 2)
    return (min(STAGE_SLOTS, k_rows // rows), rows, width)


def _load_weight(src_hbm, dst, stage, sem, col0=0, transpose=False):
    n_slots, rows, width = stage.shape
    n_blocks = src_hbm.shape[0] // rows

    def copy(k):
        return pltpu.make_async_copy(src_hbm.at[pl.ds(k * rows, rows), pl.ds(col0, width)], stage.at[k % n_slots],
                                     sem.at[k % n_slots])

    for k in range(min(n_slots - 1, n_blocks)):
        copy(k).start()
    for k in range(n_blocks):
        if k + n_slots - 1 < n_blocks:
            copy(k + n_slots - 1).start()
        copy(k).wait()
        if not transpose:
            dst[k * rows:(k + 1) * rows, 0:width] = stage[k % n_slots].astype(BF16)
            continue
        for r in range(0, rows, LANES):
            for c in range(0, width, LANES):
                dst[c:c + LANES, k * rows + r:k * rows + r + LANES] = (
                    stage[k % n_slots, r:r + LANES, c:c + LANES].T.astype(BF16))


def _first_step():
    return (pl.program_id(0) == 0) & (pl.program_id(1) == 0)


def _even_kernel(x_ref, g_ref, w_in_hbm, w_pool_hbm, ps_ref, conv_ref, w_out_hbm, o_ref,
                 ubuf, s2buf, s4buf, cbuf, w_in_ref, w_pool_ref, w_out_ref, stage_in, stage_pool, stage_out,
                 sem_in, sem_pool, sem_out, *, tm, tc):
    t = pl.program_id(1)
    n_slabs = D_MODEL // LANES
    slabs_per_group = POOL_GC // LANES
    n_s2, n_s4 = n_slabs - slabs_per_group, n_slabs - 2 * slabs_per_group
    ext = POOL_CARRY + tc

    @pl.when(_first_step())
    def _():
        _load_weight(w_in_hbm, w_in_ref, stage_in, sem_in)
        _load_weight(w_out_hbm, w_out_ref, stage_out, sem_out)
        _load_weight(w_pool_hbm, w_pool_ref, stage_pool, sem_pool)

    @pl.when(t == 0)
    def _():
        ubuf[:, 0:POOL_CARRY, :] = jnp.zeros((n_slabs, POOL_CARRY, LANES), F32)
        cbuf[:, 0:CONV_CARRY, :] = jnp.zeros((n_slabs, CONV_CARRY, LANES), F32)

    def chunk(c):
        r0 = c * tc
        x = x_ref[0, r0:r0 + tc, :]
        y = _rms_norm(x, g_ref[0:1, :]).astype(BF16)
        u = _dot(y, w_in_ref[:, 0:D_MODEL])
        gate_c = _dot(y, w_in_ref[:, 2 * D_MODEL:3 * D_MODEL])
        h_c = _dot(y, w_in_ref[:, 3 * D_MODEL:4 * D_MODEL])
        gate_b = _dot(y, w_in_ref[:, D_MODEL:2 * D_MODEL])
        z_b = _dot(y, w_in_ref[:, 5 * D_MODEL:6 * D_MODEL])
        z_a = _dot(y, w_in_ref[:, 4 * D_MODEL:5 * D_MODEL])

        pos = t * tm + r0 + lax.broadcasted_iota(jnp.int32, (tc, LANES), 0)
        a_parts = []
        for g, w in enumerate(POOL_WINDOWS):
            inv = 1.0 / jnp.minimum(pos + 1, w).astype(F32)
            slabs = []
            for j in range(g * slabs_per_group, (g + 1) * slabs_per_group):
                u_j = u[:, j * LANES:(j + 1) * LANES]
                ubuf[j, r0 + POOL_CARRY:r0 + ext, :] = u_j
                s = ubuf[j, r0 + 8:r0 + ext, :] + ubuf[j, r0 + 7:r0 + ext - 1, :]
                if w >= 4:
                    jj = c * n_s2 + j - slabs_per_group
                    s2buf[jj, :, :] = s
                    s = s[8:, :] + s2buf[jj, 6:ext - 10, :]
                if w >= 8:
                    jj = c * n_s4 + j - 2 * slabs_per_group
                    s4buf[jj, :, :] = s
                    s = s[8:, :] + s4buf[jj, 4:ext - 20, :]
                if w >= 16:
                    s = s[8:, :] + s[:-8, :]
                slabs.append(s[s.shape[0] - tc:, :] * inv - u_j)
            pooled = jnp.concatenate(slabs, axis=1).astype(BF16)
            a_parts.append(_dot(pooled, w_pool_ref[g * POOL_GC:(g + 1) * POOL_GC, :]))
        a = jnp.concatenate(a_parts, axis=1) * ps_ref[...]

        cu = gate_c * h_c
        cw = conv_ref[0]
        v_parts = []
        for j in range(n_slabs):
            lanes = slice(j * LANES, (j + 1) * LANES)
            cbuf[j, r0 + CONV_CARRY:r0 + CONV_CARRY + tc, :] = cu[:, lanes]
            v_parts.append(cw[2:3, lanes] * cu[:, lanes]
                           + cw[1:2, lanes] * cbuf[j, r0 + CONV_CARRY - 1:r0 + CONV_CARRY - 1 + tc, :]
                           + cw[0:1, lanes] * cbuf[j, r0 + CONV_CARRY - 2:r0 + CONV_CARRY - 2 + tc, :])
        bo = gate_b * jnp.concatenate(v_parts, axis=1)

        mix_b = (bo * _silu(z_b)).astype(BF16)
        mix_a = (a * _silu(z_a)).astype(BF16)
        o_ref[0, r0:r0 + tc, :] = x + (_dot(mix_b, w_out_ref[D_MODEL:, 0:D_MODEL])
                                       + _dot(mix_a, w_out_ref[0:D_MODEL, 0:D_MODEL]))

    for c in range(tm // tc):
        chunk(c)
    ubuf[:, 0:POOL_CARRY, :] = ubuf[:, tm:tm + POOL_CARRY, :]
    cbuf[:, 0:CONV_CARRY, :] = cbuf[:, tm:tm + CONV_CARRY, :]


def _const_spec(shape):
    nd = len(shape)
    return pl.BlockSpec(shape, lambda *_: (0,) * nd, pipeline_mode=pl.Buffered(1))


def _padded(shape):
    k_rows, width = shape
    return (k_rows, width + LANES) if (width // LANES) % 2 == 0 else (k_rows, width)


def _even_layer(x, g, w_in, w_pool, pool_scale, conv_w, w_out, *, tm, tc):
    b, s, d = x.shape
    tile = pl.BlockSpec((1, tm, d), lambda i, j: (i, j, 0))
    n_slabs, per_group = d // LANES, POOL_GC // LANES
    w_pool = w_pool.reshape(-1, w_pool.shape[-1])
    weight_bytes = 2 * (w_in.size + w_pool.size + w_out.size)
    stages = [_stage_shape(w.shape) for w in (w_in, w_pool, w_out)]
    stage_bytes = sum(4 * math.prod(st) for st in stages)
    tile_bytes, chunk_bytes = 4 * tm * d, 4 * tc * d
    assert weight_bytes + stage_bytes + (4 + 2) * tile_bytes + (3 + 10) * chunk_bytes <= VMEM_BYTES
    vmem_limit = VMEM_BYTES
    hbm = pl.BlockSpec(memory_space=pl.ANY)
    return pl.pallas_call(
        functools.partial(_even_kernel, tm=tm, tc=tc),
        grid=(b, s // tm),
        in_specs=[tile, _const_spec(g.shape), hbm, hbm, _const_spec((1, d)), _const_spec(conv_w.shape), hbm],
        out_specs=tile,
        out_shape=jax.ShapeDtypeStruct(x.shape, x.dtype),
        scratch_shapes=[pltpu.VMEM((n_slabs, POOL_CARRY + tm, LANES), F32),
                        pltpu.VMEM((tm // tc * (n_slabs - per_group), POOL_CARRY + tc - 8, LANES), F32),
                        pltpu.VMEM((tm // tc * (n_slabs - 2 * per_group), POOL_CARRY + tc - 16, LANES), F32),
                        pltpu.VMEM((n_slabs, CONV_CARRY + tm, LANES), F32),
                        pltpu.VMEM(_padded(w_in.shape), BF16),
                        pltpu.VMEM(w_pool.shape, BF16),
                        pltpu.VMEM(_padded(w_out.shape), BF16),
                        *[pltpu.VMEM(st, F32) for st in stages],
                        *[pltpu.SemaphoreType.DMA((st[0],)) for st in stages]],
        compiler_params=pltpu.CompilerParams(
            dimension_semantics=("arbitrary", "arbitrary"), vmem_limit_bytes=vmem_limit),
        name="even_layer",
    )(x, g, w_in, w_pool, pool_scale.reshape(1, d), conv_w, w_out)


def _odd_kernel(sink_ref, h_ref, pos_ref, g_ref, w_in_hbm, b_in_ref, w_out_hbm, b_out_ref, gf_ref,
                o_ref, kx, vtx, b_qkv_ref, w_qkv_ref, w_z_ref, w_out_ref, stage_qkv, stage_z, stage_out,
                sem_qkv, sem_z, sem_out, *, tm, tc):
    t = pl.program_id(1)
    nq = tc // WINDOW
    qkv_w = D_MODEL + 2 * KV_WIDTH

    @pl.when(_first_step())
    def _():
        for c0 in range(0, qkv_w, LANES):
            b_qkv_ref[c0:c0 + LANES, :] = jnp.broadcast_to(b_in_ref[:, c0:c0 + LANES], (LANES, LANES)).T
        _load_weight(w_in_hbm, w_qkv_ref, stage_qkv, sem_qkv, transpose=True)
        _load_weight(w_in_hbm, w_z_ref, stage_z, sem_z, col0=qkv_w)
        _load_weight(w_out_hbm, w_out_ref, stage_out, sem_out)

    @pl.when(t == 0)
    def _():
        kx[0:WINDOW, :] = jnp.zeros((WINDOW, LANES), BF16)
        vtx[:, 0:WINDOW] = jnp.zeros((KV_WIDTH, WINDOW), BF16)

    slot = lax.broadcasted_iota(jnp.int32, (WINDOW, LANES), 0)
    qidx = lax.broadcasted_iota(jnp.int32, (WINDOW, LANES), 1)
    diff = slot - qidx
    eye = jnp.where(diff == 0, 1.0, 0.0).astype(BF16)
    masked = jnp.full((WINDOW, LANES), -MASK_BIAS, F32)
    bias_prev = jnp.where(diff > 0, 0.0, masked)
    bias_cur = jnp.where(diff <= 0, 0.0, masked)
    bias_mid = jnp.concatenate([bias_prev, bias_cur], axis=0).astype(BF16)
    bias_first = jnp.concatenate([jnp.where(t == 0, masked, bias_prev), bias_cur], axis=0).astype(BF16)
    ones_rows = jnp.ones((2 * SUBLANES, 2 * WINDOW), BF16)
    no_head = jnp.zeros((HEAD_DIM, WINDOW), BF16)

    def rotary(x_t, cos8, sin8, scale):
        x1, x2 = x_t[0:N_FREQ, :], x_t[N_FREQ:ROT_DIMS, :]
        rest = x_t[ROT_DIMS:, :]
        return [x1 * cos8 - x2 * sin8, x2 * cos8 + x1 * sin8, rest if scale == 1.0 else rest * scale]

    def chunk(c):
        r0 = c * tc
        h = h_ref[0, r0:r0 + tc, :]
        y = _rms_norm(h, g_ref[1:2, :]).astype(BF16)
        qkv_t = lax.dot_general(w_qkv_ref[:, 0:D_MODEL], y, (((1,), (1,)), ((), ())), preferred_element_type=F32)
        qkv_t = qkv_t + jnp.concatenate([b_qkv_ref[...]] * (tc // LANES), axis=1)

        cos8, sin8 = _rope_angles(pos_ref[pl.ds(pl.program_id(0), 1), r0:r0 + tc], tc)
        cq8, sq8 = cos8 * Q_SCALE, sin8 * Q_SCALE
        q_rows = []
        for hd in range(N_HEADS):
            q_rows += rotary(qkv_t[hd * HEAD_DIM:(hd + 1) * HEAD_DIM, :], cq8, sq8, Q_SCALE)
        qr_t = jnp.concatenate(q_rows, axis=0).astype(BF16)
        k_rows = []
        for kv in range(N_KV_HEADS):
            k_rows += rotary(qkv_t[D_MODEL + kv * HEAD_DIM:D_MODEL + (kv + 1) * HEAD_DIM, :], cos8, sin8, 1.0)
        kr_t = jnp.concatenate(k_rows, axis=0)
        k0, k1 = WINDOW + r0, WINDOW + r0 + tc
        for blk in range(nq):
            kx[k0 + blk * WINDOW:k0 + (blk + 1) * WINDOW, :] = kr_t[:, blk * WINDOW:(blk + 1) * WINDOW].T.astype(BF16)
        vtx[:, k0:k1] = qkv_t[D_MODEL + KV_WIDTH:qkv_w, :].astype(BF16)

        out_chunks = [[None] * nq for _ in range(D_MODEL // LANES)]
        units = [(qb, kv) for qb in range(nq) for kv in range(N_KV_HEADS)]

        def scores(qb, kv):
            a0 = r0 + qb * WINDOW
            bias = bias_first if a0 == 0 else bias_mid
            w_ext = jnp.concatenate([kx[a0:a0 + 2 * WINDOW, :], bias], axis=1)
            cols = []
            for g in range(GROUP):
                hd = kv * GROUP + g
                q_h = qr_t[hd * HEAD_DIM:(hd + 1) * HEAD_DIM, qb * WINDOW:(qb + 1) * WINDOW]
                cols.append(jnp.concatenate([q_h, no_head, eye] if kv == 0 else [no_head, q_h, eye], axis=0))
            return _dot(w_ext, jnp.concatenate(cols, axis=1))

        def attend(qb, kv, st):
            a0 = r0 + qb * WINDOW
            blocks = []
            terms = []
            for g in range(GROUP):
                sink = sink_ref[kv * GROUP + g] * LOG2E
                blk_s = st[:, g * WINDOW:(g + 1) * WINDOW]
                m = jnp.maximum(jnp.max(blk_s, axis=0, keepdims=True), sink)
                blocks.append(jnp.exp2(blk_s - m).astype(BF16))
                terms.append(jnp.exp2(sink - m))
            p_t = jnp.concatenate(blocks, axis=1)
            v_win = vtx[kv * HEAD_DIM:(kv + 1) * HEAD_DIM, a0:a0 + 2 * WINDOW]
            o_t = _dot(jnp.concatenate([v_win, ones_rows], axis=0), p_t)
            den = o_t[HEAD_DIM:HEAD_DIM + SUBLANES, :] + jnp.concatenate(terms, axis=1)
            n_t = o_t[:HEAD_DIM, :] * jnp.concatenate([1.0 / den] * (HEAD_DIM // SUBLANES), axis=0)
            for j in range(PAIRS_PER_KV):
                pair = jnp.concatenate([n_t[:, 2 * j * WINDOW:(2 * j + 1) * WINDOW],
                                        n_t[:, (2 * j + 1) * WINDOW:(2 * j + 2) * WINDOW]], axis=0)
                out_chunks[kv * PAIRS_PER_KV + j][qb] = pair.T

        n_pieces = D_MODEL // GATE_COLS
        every = max(1, len(units) // n_pieces)
        z_parts = []

        def gate_piece():
            c0 = len(z_parts) * GATE_COLS
            z_parts.append(_dot(y, w_z_ref[:, c0:c0 + GATE_COLS]) + b_in_ref[:, qkv_w + c0:qkv_w + c0 + GATE_COLS])

        gate_piece()
        st_next = scores(*units[0])
        for i, unit in enumerate(units):
            st = st_next
            if i + 1 < len(units):
                if (i + 1) % every == 0 and len(z_parts) < n_pieces:
                    gate_piece()
                st_next = scores(*units[i + 1])
            attend(*unit, st)
        while len(z_parts) < n_pieces:
            gate_piece()

        z = jnp.concatenate(z_parts, axis=1)
        attn = jnp.concatenate([jnp.concatenate(col, axis=0) for col in out_chunks], axis=1)
        gated = (attn * _silu(z)).astype(BF16)
        for p0 in range(0, tc, OUT_ROWS):
            h2 = h[p0:p0 + OUT_ROWS, :] + _dot(gated[p0:p0 + OUT_ROWS, :], w_out_ref[:, 0:D_MODEL]) + b_out_ref[...]
            o_ref[0, r0 + p0:r0 + p0 + OUT_ROWS, :] = _rms_norm(h2, gf_ref[...])

    for c in range(tm // tc):
        chunk(c)
    kx[0:WINDOW, :] = kx[tm:tm + WINDOW, :]
    vtx[:, 0:WINDOW] = vtx[:, tm:tm + WINDOW]


def _odd_layer(h, positions, g, w_in, b_in, sinks, w_out, b_out, g_final, *, tm, tc):
    b, s, d = h.shape
    tile = pl.BlockSpec((1, tm, d), lambda i, j: (i, j, 0))
    pos_spec = pl.BlockSpec((b, tm), lambda i, j: (0, j))
    qkv_w = d + 2 * KV_WIDTH
    qkv_shape = (w_in.shape[0], qkv_w)
    z_shape = (w_in.shape[0], w_in.shape[1] - qkv_w)
    weight_bytes = 2 * (w_in.size + w_out.size)
    stages = [_stage_shape(shape) for shape in (qkv_shape, z_shape, w_out.shape)]
    stage_bytes = sum(4 * math.prod(st) for st in stages)
    tile_bytes = 4 * tm * d
    vmem_limit = weight_bytes + stage_bytes + (4 + 12) * tile_bytes + (8 << 20)
    hbm = pl.BlockSpec(memory_space=pl.ANY)
    return pl.pallas_call(
        functools.partial(_odd_kernel, tm=tm, tc=tc),
        grid=(b, s // tm),
        in_specs=[pl.BlockSpec(memory_space=pltpu.SMEM), tile, pos_spec, _const_spec(g.shape),
                  hbm, _const_spec((1, ODD_IN)), hbm, _const_spec((1, d)), _const_spec((1, d))],
        out_specs=tile,
        out_shape=jax.ShapeDtypeStruct(h.shape, h.dtype),
        scratch_shapes=[pltpu.VMEM((WINDOW + tm, LANES), BF16),
                        pltpu.VMEM((KV_WIDTH, WINDOW + tm), BF16),
                        pltpu.VMEM((qkv_w, LANES), F32),
                        pltpu.VMEM(_padded(qkv_shape[::-1]), BF16),
                        pltpu.VMEM(_padded(z_shape), BF16),
                        pltpu.VMEM(_padded(w_out.shape), BF16),
                        *[pltpu.VMEM(st, F32) for st in stages],
                        *[pltpu.SemaphoreType.DMA((st[0],)) for st in stages]],
        compiler_params=pltpu.CompilerParams(
            dimension_semantics=("arbitrary", "arbitrary"), vmem_limit_bytes=vmem_limit),
        name="odd_layer",
    )(sinks, h, positions, g, w_in, b_in.reshape(1, ODD_IN), w_out,
      b_out.reshape(1, d), g_final.reshape(1, d))


EVEN_TM = 1024
EVEN_TC = 256
ODD_TM = 1024
ODD_TC = 512


def kernel(x, positions, norm_g, w_in_even, w_pool, pool_scale, conv_w, w_out_even, w_in_odd, b_in_odd,
           attn_sinks, w_out_odd, b_out_odd, final_norm_g):
    assert norm_g.shape[0] == 2 and w_in_even.shape[0] == 1 and w_in_odd.shape[0] == 1
    h = _even_layer(x, norm_g, w_in_even[0], w_pool[0], pool_scale[0], conv_w, w_out_even[0],
                    tm=EVEN_TM, tc=EVEN_TC)
    return _odd_layer(h, positions, norm_g, w_in_odd[0], b_in_odd[0], attn_sinks[0], w_out_odd[0], b_out_odd[0],
                      final_norm_g, tm=ODD_TM, tc=ODD_TC)
```

```python
import functools
import math

import jax
import jax.numpy as jnp
from jax import lax
from jax.experimental import pallas as pl
from jax.experimental.pallas import tpu as pltpu

D_MODEL = 1024
NORM_EPS = 1e-5
POOL_WINDOWS = (2, 4, 8, 16)
POOL_GC = D_MODEL // len(POOL_WINDOWS)
CONV_WIDTH = D_MODEL
EVEN_IN = 6 * D_MODEL
HEAD_DIM = 64
N_HEADS = 16
N_KV_HEADS = 2
GROUP = N_HEADS // N_KV_HEADS
KV_WIDTH = N_KV_HEADS * HEAD_DIM
ODD_IN = 2 * D_MODEL + 2 * KV_WIDTH
WINDOW = 128
ROPE_THETA = 500000.0
ROT_DIMS = HEAD_DIM // 4
N_FREQ = ROT_DIMS // 2

VMEM_BYTES = 64 << 20
LANES = 128
SUBLANES = 8
POOL_CARRY = 32
CONV_CARRY = SUBLANES
PAIRS_PER_KV = GROUP // 2
LOG2E = math.log2(math.e)
Q_SCALE = HEAD_DIM ** -0.5 * LOG2E
MASK_BIAS = 2.0 ** 100
STAGE_SLOTS = 4
STAGE_BLOCK_BYTES = 1 << 20
OUT_ROWS = 512
GATE_COLS = 256

_INV_FREQ = tuple(float(ROPE_THETA ** (-(2.0 * i) / ROT_DIMS)) for i in range(N_FREQ))

F32 = jnp.float32
BF16 = jnp.bfloat16


def _dot(a, b):
    return jnp.dot(a, b, preferred_element_type=F32)


def _rms_norm(x, g):
    ms = jnp.mean(x * x, axis=-1, keepdims=True)
    return x * lax.rsqrt(ms + NORM_EPS) * g


def _silu(z):
    return z * jax.nn.sigmoid(z)


def _rope_angles(pos_row, tm):
    sub = lax.broadcasted_iota(jnp.int32, (N_FREQ, tm), 0)
    inv_freq = jnp.full((N_FREQ, tm), _INV_FREQ[N_FREQ - 1], F32)
    for f in range(N_FREQ - 1):
        inv_freq = jnp.where(sub == f, _INV_FREQ[f], inv_freq)
    ang = pos_row.astype(F32) * inv_freq
    return jnp.cos(ang), jnp.sin(ang)


def _stage_shape(weight_shape):
    k_rows, width = weight_shape
    rows = min(k_rows, pl.next_power_of_2(STAGE_BLOCK_BYTES // (4 * width) + 1) // 2)
    return (min(STAGE_SLOTS, k_rows // rows), rows, width)


def _load_weights(*jobs):
    tasks = []
    for j, (src_hbm, _, stage, _, _, _) in enumerate(jobs):
        tasks += [(j, k) for k in range(src_hbm.shape[0] // stage.shape[1])]
    ahead = STAGE_SLOTS - 1

    def copy(i):
        j, k = tasks[i]
        src_hbm, _, stage, sem, col0, _ = jobs[j]
        n_slots, rows, width = stage.shape
        return pltpu.make_async_copy(src_hbm.at[pl.ds(k * rows, rows), pl.ds(col0, width)], stage.at[k % n_slots],
                                     sem.at[k % n_slots])

    for i in range(min(ahead, len(tasks))):
        copy(i).start(priority=i % 2)
    for i, (j, k) in enumerate(tasks):
        nxt = i + ahead
        if nxt < len(tasks):
            nj, nk = tasks[nxt]
            reused = (nj, nk - jobs[nj][2].shape[0])
            assert reused[1] < 0 or tasks.index(reused) < i
            copy(nxt).start(priority=nxt % 2)
        copy(i).wait()
        _, dst, stage, _, _, transpose = jobs[j]
        n_slots, rows, width = stage.shape
        if not transpose:
            dst[k * rows:(k + 1) * rows, 0:width] = stage[k % n_slots].astype(BF16)
            continue
        for r in range(0, rows, LANES):
            for c in range(0, width, LANES):
                dst[c:c + LANES, k * rows + r:k * rows + r + LANES] = (
                    stage[k % n_slots, r:r + LANES, c:c + LANES].T.astype(BF16))


def _first_step():
    return (pl.program_id(0) == 0) & (pl.program_id(1) == 0)


def _even_kernel(x_ref, g_ref, w_in_hbm, w_pool_hbm, ps_ref, conv_ref, w_out_hbm, o_ref,
                 ubuf, s2buf, s4buf, cbuf, w_in_ref, w_pool_ref, w_out_ref, stage_in, stage_pool, stage_out,
                 sem_in, sem_pool, sem_out, *, tm, tc):
    t = pl.program_id(1)
    n_slabs = D_MODEL // LANES
    slabs_per_group = POOL_GC // LANES
    n_s2, n_s4 = n_slabs - slabs_per_group, n_slabs - 2 * slabs_per_group
    ext = POOL_CARRY + tc

    @pl.when(_first_step())
    def _():
        _load_weights((w_in_hbm, w_in_ref, stage_in, sem_in, 0, False),
                      (w_out_hbm, w_out_ref, stage_out, sem_out, 0, False),
                      (w_pool_hbm, w_pool_ref, stage_pool, sem_pool, 0, False))

    @pl.when(t == 0)
    def _():
        ubuf[:, 0:POOL_CARRY, :] = jnp.zeros((n_slabs, POOL_CARRY, LANES), F32)
        cbuf[:, 0:CONV_CARRY, :] = jnp.zeros((n_slabs, CONV_CARRY, LANES), F32)

    def chunk(c):
        r0 = c * tc
        x = x_ref[0, r0:r0 + tc, :]
        y = _rms_norm(x, g_ref[0:1, :]).astype(BF16)
        u = _dot(y, w_in_ref[:, 0:D_MODEL])
        gate_c = _dot(y, w_in_ref[:, 2 * D_MODEL:3 * D_MODEL])
        h_c = _dot(y, w_in_ref[:, 3 * D_MODEL:4 * D_MODEL])
        gate_b = _dot(y, w_in_ref[:, D_MODEL:2 * D_MODEL])
        z_b = _dot(y, w_in_ref[:, 5 * D_MODEL:6 * D_MODEL])
        z_a = _dot(y, w_in_ref[:, 4 * D_MODEL:5 * D_MODEL])

        pos = t * tm + r0 + lax.broadcasted_iota(jnp.int32, (tc, LANES), 0)
        a_parts = []
        for g, w in enumerate(POOL_WINDOWS):
            inv = 1.0 / jnp.minimum(pos + 1, w).astype(F32)
            slabs = []
            for j in range(g * slabs_per_group, (g + 1) * slabs_per_group):
                u_j = u[:, j * LANES:(j + 1) * LANES]
                ubuf[j, r0 + POOL_CARRY:r0 + ext, :] = u_j
                s = ubuf[j, r0 + 8:r0 + ext, :] + ubuf[j, r0 + 7:r0 + ext - 1, :]
                if w >= 4:
                    jj = c * n_s2 + j - slabs_per_group
                    s2buf[jj, :, :] = s
                    s = s[8:, :] + s2buf[jj, 6:ext - 10, :]
                if w >= 8:
                    jj = c * n_s4 + j - 2 * slabs_per_group
                    s4buf[jj, :, :] = s
                    s = s[8:, :] + s4buf[jj, 4:ext - 20, :]
                if w >= 16:
                    s = s[8:, :] + s[:-8, :]
                slabs.append(s[s.shape[0] - tc:, :] * inv - u_j)
            pooled = jnp.concatenate(slabs, axis=1).astype(BF16)
            a_parts.append(_dot(pooled, w_pool_ref[g * POOL_GC:(g + 1) * POOL_GC, :]))
        a = jnp.concatenate(a_parts, axis=1) * ps_ref[...]

        cu = gate_c * h_c
        cw = conv_ref[0]
        v_parts = []
        for j in range(n_slabs):
            lanes = slice(j * LANES, (j + 1) * LANES)
            cbuf[j, r0 + CONV_CARRY:r0 + CONV_CARRY + tc, :] = cu[:, lanes]
            v_parts.append(cw[2:3, lanes] * cu[:, lanes]
                           + cw[1:2, lanes] * cbuf[j, r0 + CONV_CARRY - 1:r0 + CONV_CARRY - 1 + tc, :]
                           + cw[0:1, lanes] * cbuf[j, r0 + CONV_CARRY - 2:r0 + CONV_CARRY - 2 + tc, :])
        bo = gate_b * jnp.concatenate(v_parts, axis=1)

        mix_b = (bo * _silu(z_b)).astype(BF16)
        mix_a = (a * _silu(z_a)).astype(BF16)
        o_ref[0, r0:r0 + tc, :] = x + (_dot(mix_b, w_out_ref[D_MODEL:, 0:D_MODEL])
                                       + _dot(mix_a, w_out_ref[0:D_MODEL, 0:D_MODEL]))

    for c in range(tm // tc):
        chunk(c)
    ubuf[:, 0:POOL_CARRY, :] = ubuf[:, tm:tm + POOL_CARRY, :]
    cbuf[:, 0:CONV_CARRY, :] = cbuf[:, tm:tm + CONV_CARRY, :]


def _const_spec(shape):
    nd = len(shape)
    return pl.BlockSpec(shape, lambda *_: (0,) * nd, pipeline_mode=pl.Buffered(1))


def _padded(shape):
    k_rows, width = shape
    return (k_rows, width + LANES) if (width // LANES) % 2 == 0 else (k_rows, width)


def _even_layer(x, g, w_in, w_pool, pool_scale, conv_w, w_out, *, tm, tc):
    b, s, d = x.shape
    tile = pl.BlockSpec((1, tm, d), lambda i, j: (i, j, 0))
    n_slabs, per_group = d // LANES, POOL_GC // LANES
    w_pool = w_pool.reshape(-1, w_pool.shape[-1])
    weight_bytes = 2 * (w_in.size + w_pool.size + w_out.size)
    stages = [_stage_shape(w.shape) for w in (w_in, w_pool, w_out)]
    stage_bytes = sum(4 * math.prod(st) for st in stages)
    tile_bytes, chunk_bytes = 4 * tm * d, 4 * tc * d
    assert weight_bytes + stage_bytes + (4 + 2) * tile_bytes + (3 + 10) * chunk_bytes <= VMEM_BYTES
    vmem_limit = VMEM_BYTES
    hbm = pl.BlockSpec(memory_space=pl.ANY)
    return pl.pallas_call(
        functools.partial(_even_kernel, tm=tm, tc=tc),
        grid=(b, s // tm),
        in_specs=[tile, _const_spec(g.shape), hbm, hbm, _const_spec((1, d)), _const_spec(conv_w.shape), hbm],
        out_specs=tile,
        out_shape=jax.ShapeDtypeStruct(x.shape, x.dtype),
        scratch_shapes=[pltpu.VMEM((n_slabs, POOL_CARRY + tm, LANES), F32),
                        pltpu.VMEM((tm // tc * (n_slabs - per_group), POOL_CARRY + tc - 8, LANES), F32),
                        pltpu.VMEM((tm // tc * (n_slabs - 2 * per_group), POOL_CARRY + tc - 16, LANES), F32),
                        pltpu.VMEM((n_slabs, CONV_CARRY + tm, LANES), F32),
                        pltpu.VMEM(_padded(w_in.shape), BF16),
                        pltpu.VMEM(w_pool.shape, BF16),
                        pltpu.VMEM(_padded(w_out.shape), BF16),
                        *[pltpu.VMEM(st, F32) for st in stages],
                        *[pltpu.SemaphoreType.DMA((st[0],)) for st in stages]],
        compiler_params=pltpu.CompilerParams(
            dimension_semantics=("arbitrary", "arbitrary"), vmem_limit_bytes=vmem_limit),
        name="even_layer",
    )(x, g, w_in, w_pool, pool_scale.reshape(1, d), conv_w, w_out)


def _odd_kernel(sink_ref, h_ref, pos_ref, g_ref, w_in_hbm, b_in_ref, w_out_hbm, b_out_ref, gf_ref,
                o_ref, kx, vtx, b_qkv_ref, w_qkv_ref, w_z_ref, w_out_ref, stage_qkv, stage_z, stage_out,
                sem_qkv, sem_z, sem_out, *, tm, tc):
    t = pl.program_id(1)
    nq = tc // WINDOW
    qkv_w = D_MODEL + 2 * KV_WIDTH

    @pl.when(_first_step())
    def _():
        for c0 in range(0, qkv_w, LANES):
            b_qkv_ref[c0:c0 + LANES, :] = jnp.broadcast_to(b_in_ref[:, c0:c0 + LANES], (LANES, LANES)).T
        _load_weights((w_in_hbm, w_qkv_ref, stage_qkv, sem_qkv, 0, True),
                      (w_in_hbm, w_z_ref, stage_z, sem_z, qkv_w, False),
                      (w_out_hbm, w_out_ref, stage_out, sem_out, 0, False))

    @pl.when(t == 0)
    def _():
        kx[0:WINDOW, :] = jnp.zeros((WINDOW, LANES), BF16)
        vtx[:, 0:WINDOW] = jnp.zeros((KV_WIDTH, WINDOW), BF16)

    slot = lax.broadcasted_iota(jnp.int32, (WINDOW, LANES), 0)
    qidx = lax.broadcasted_iota(jnp.int32, (WINDOW, LANES), 1)
    diff = slot - qidx
    eye = jnp.where(diff == 0, 1.0, 0.0).astype(BF16)
    masked = jnp.full((WINDOW, LANES), -MASK_BIAS, F32)
    bias_prev = jnp.where(diff > 0, 0.0, masked)
    bias_cur = jnp.where(diff <= 0, 0.0, masked)
    bias_mid = jnp.concatenate([bias_prev, bias_cur], axis=0).astype(BF16)
    bias_first = jnp.concatenate([jnp.where(t == 0, masked, bias_prev), bias_cur], axis=0).astype(BF16)
    ones_rows = jnp.ones((2 * SUBLANES, 2 * WINDOW), BF16)
    no_head = jnp.zeros((HEAD_DIM, WINDOW), BF16)

    def rotary(x_t, cos8, sin8, scale):
        x1, x2 = x_t[0:N_FREQ, :], x_t[N_FREQ:ROT_DIMS, :]
        rest = x_t[ROT_DIMS:, :]
        return [x1 * cos8 - x2 * sin8, x2 * cos8 + x1 * sin8, rest if scale == 1.0 else rest * scale]

    def chunk(c):
        r0 = c * tc
        h = h_ref[0, r0:r0 + tc, :]
        y = _rms_norm(h, g_ref[1:2, :]).astype(BF16)
        qkv_t = lax.dot_general(w_qkv_ref[:, 0:D_MODEL], y, (((1,), (1,)), ((), ())), preferred_element_type=F32)
        qkv_t = qkv_t + jnp.concatenate([b_qkv_ref[...]] * (tc // LANES), axis=1)

        cos8, sin8 = _rope_angles(pos_ref[pl.ds(pl.program_id(0), 1), r0:r0 + tc], tc)
        cq8, sq8 = cos8 * Q_SCALE, sin8 * Q_SCALE
        q_rows = []
        for hd in range(N_HEADS):
            q_rows += rotary(qkv_t[hd * HEAD_DIM:(hd + 1) * HEAD_DIM, :], cq8, sq8, Q_SCALE)
        qr_t = jnp.concatenate(q_rows, axis=0).astype(BF16)
        k_rows = []
        for kv in range(N_KV_HEADS):
            k_rows += rotary(qkv_t[D_MODEL + kv * HEAD_DIM:D_MODEL + (kv + 1) * HEAD_DIM, :], cos8, sin8, 1.0)
        kr_t = jnp.concatenate(k_rows, axis=0)
        k0, k1 = WINDOW + r0, WINDOW + r0 + tc
        for blk in range(nq):
            kx[k0 + blk * WINDOW:k0 + (blk + 1) * WINDOW, :] = kr_t[:, blk * WINDOW:(blk + 1) * WINDOW].T.astype(BF16)
        vtx[:, k0:k1] = qkv_t[D_MODEL + KV_WIDTH:qkv_w, :].astype(BF16)

        out_chunks = [[None] * nq for _ in range(D_MODEL // LANES)]
        units = [(qb, kv) for qb in range(nq) for kv in range(N_KV_HEADS)]

        def scores(qb, kv):
            a0 = r0 + qb * WINDOW
            bias = bias_first if a0 == 0 else bias_mid
            w_ext = jnp.concatenate([kx[a0:a0 + 2 * WINDOW, :], bias], axis=1)
            cols = []
            for g in range(GROUP):
                hd = kv * GROUP + g
                q_h = qr_t[hd * HEAD_DIM:(hd + 1) * HEAD_DIM, qb * WINDOW:(qb + 1) * WINDOW]
                cols.append(jnp.concatenate([q_h, no_head, eye] if kv == 0 else [no_head, q_h, eye], axis=0))
            return _dot(w_ext, jnp.concatenate(cols, axis=1))

        def attend(qb, kv, st):
            a0 = r0 + qb * WINDOW
            blocks = []
            terms = []
            for g in range(GROUP):
                sink = sink_ref[kv * GROUP + g] * LOG2E
                blk_s = st[:, g * WINDOW:(g + 1) * WINDOW]
                m = jnp.maximum(jnp.max(blk_s, axis=0, keepdims=True), sink)
                blocks.append(jnp.exp2(blk_s - m).astype(BF16))
                terms.append(jnp.exp2(sink - m))
            p_t = jnp.concatenate(blocks, axis=1)
            v_win = vtx[kv * HEAD_DIM:(kv + 1) * HEAD_DIM, a0:a0 + 2 * WINDOW]
            o_t = _dot(jnp.concatenate([v_win, ones_rows], axis=0), p_t)
            den = o_t[HEAD_DIM:HEAD_DIM + SUBLANES, :] + jnp.concatenate(terms, axis=1)
            n_t = o_t[:HEAD_DIM, :] * jnp.concatenate([1.0 / den] * (HEAD_DIM // SUBLANES), axis=0)
            for j in range(PAIRS_PER_KV):
                pair = jnp.concatenate([n_t[:, 2 * j * WINDOW:(2 * j + 1) * WINDOW],
                                        n_t[:, (2 * j + 1) * WINDOW:(2 * j + 2) * WINDOW]], axis=0)
                out_chunks[kv * PAIRS_PER_KV + j][qb] = pair.T

        n_pieces = D_MODEL // GATE_COLS
        every = max(1, len(units) // n_pieces)
        z_parts = []

        def gate_piece():
            c0 = len(z_parts) * GATE_COLS
            z_parts.append(_dot(y, w_z_ref[:, c0:c0 + GATE_COLS]) + b_in_ref[:, qkv_w + c0:qkv_w + c0 + GATE_COLS])

        gate_piece()
        st_next = scores(*units[0])
        for i, unit in enumerate(units):
            st = st_next
            if i + 1 < len(units):
                if (i + 1) % every == 0 and len(z_parts) < n_pieces:
                    gate_piece()
                st_next = scores(*units[i + 1])
            attend(*unit, st)
        while len(z_parts) < n_pieces:
            gate_piece()

        z = jnp.concatenate(z_parts, axis=1)
        attn = jnp.concatenate([jnp.concatenate(col, axis=0) for col in out_chunks], axis=1)
        gated = (attn * _silu(z)).astype(BF16)
        for p0 in range(0, tc, OUT_ROWS):
            h2 = h[p0:p0 + OUT_ROWS, :] + _dot(gated[p0:p0 + OUT_ROWS, :], w_out_ref[:, 0:D_MODEL]) + b_out_ref[...]
            o_ref[0, r0 + p0:r0 + p0 + OUT_ROWS, :] = _rms_norm(h2, gf_ref[...])

    for c in range(tm // tc):
        chunk(c)
    kx[0:WINDOW, :] = kx[tm:tm + WINDOW, :]
    vtx[:, 0:WINDOW] = vtx[:, tm:tm + WINDOW]


def _odd_layer(h, positions, g, w_in, b_in, sinks, w_out, b_out, g_final, *, tm, tc):
    b, s, d = h.shape
    tile = pl.BlockSpec((1, tm, d), lambda i, j: (i, j, 0))
    pos_spec = pl.BlockSpec((b, tm), lambda i, j: (0, j))
    qkv_w = d + 2 * KV_WIDTH
    qkv_shape = (w_in.shape[0], qkv_w)
    z_shape = (w_in.shape[0], w_in.shape[1] - qkv_w)
    weight_bytes = 2 * (w_in.size + w_out.size)
    stages = [_stage_shape(shape) for shape in (qkv_shape, z_shape, w_out.shape)]
    stage_bytes = sum(4 * math.prod(st) for st in stages)
    tile_bytes = 4 * tm * d
    vmem_limit = weight_bytes + stage_bytes + (4 + 12) * tile_bytes + (8 << 20)
    hbm = pl.BlockSpec(memory_space=pl.ANY)
    return pl.pallas_call(
        functools.partial(_odd_kernel, tm=tm, tc=tc),
        grid=(b, s // tm),
        in_specs=[pl.BlockSpec(memory_space=pltpu.SMEM), tile, pos_spec, _const_spec(g.shape),
                  hbm, _const_spec((1, ODD_IN)), hbm, _const_spec((1, d)), _const_spec((1, d))],
        out_specs=tile,
        out_shape=jax.ShapeDtypeStruct(h.shape, h.dtype),
        scratch_shapes=[pltpu.VMEM((WINDOW + tm, LANES), BF16),
                        pltpu.VMEM((KV_WIDTH, WINDOW + tm), BF16),
                        pltpu.VMEM((qkv_w, LANES), F32),
                        pltpu.VMEM(_padded(qkv_shape[::-1]), BF16),
                        pltpu.VMEM(_padded(z_shape), BF16),
                        pltpu.VMEM(_padded(w_out.shape), BF16),
                        *[pltpu.VMEM(st, F32) for st in stages],
                        *[pltpu.SemaphoreType.DMA((st[0],)) for st in stages]],
        compiler_params=pltpu.CompilerParams(
            dimension_semantics=("arbitrary", "arbitrary"), vmem_limit_bytes=vmem_limit),
        name="odd_layer",
    )(sinks, h, positions, g, w_in, b_in.reshape(1, ODD_IN), w_out,
      b_out.reshape(1, d), g_final.reshape(1, d))


EVEN_TM = 1024
EVEN_TC = 256
ODD_TM = 1024
ODD_TC = 512


def kernel(x, positions, norm_g, w_in_even, w_pool, pool_scale, conv_w, w_out_even, w_in_odd, b_in_odd,
           attn_sinks, w_out_odd, b_out_odd, final_norm_g):
    assert norm_g.shape[0] == 2 and w_in_even.shape[0] == 1 and w_in_odd.shape[0] == 1
    h = _even_layer(x, norm_g, w_in_even[0], w_pool[0], pool_scale[0], conv_w, w_out_even[0],
                    tm=EVEN_TM, tc=EVEN_TC)
    return _odd_layer(h, positions, norm_g, w_in_odd[0], b_in_odd[0], attn_sinks[0], w_out_odd[0], b_out_odd[0],
                      final_norm_g, tm=ODD_TM, tc=ODD_TC)
```
